```python
import math
import jax, jax.numpy as jnp
from jax import lax
import numpy as np

D_MODEL = 1024
BATCH = 4
SEQ = 8192
DEPTH = 4

N_META = 16
MLA_HEADS = 8
Q_LORA = 384
KV_LORA = 256
QK_NOPE = 64
QK_ROPE = 32
V_HEAD = 64
ROPE_THETA = 10000.0
Q_BLOCK = 128
HG_HEADS = 4
HG_KDIM = 128
HG_VDIM = 128
HG_CHUNK = 64
D_FF = 2816
EPS = 1e-6
NEG_BIG = -1e30
F_MIN = 1e-20

MLA_WIDTH = MLA_HEADS * V_HEAD
HG_FWIDTH = HG_HEADS * HG_KDIM
HG_WIDTH = HG_HEADS * HG_VDIM
IN_SPLITS = (Q_LORA, KV_LORA, QK_ROPE, HG_FWIDTH, HG_FWIDTH, HG_WIDTH, HG_WIDTH, D_MODEL, D_MODEL)
D_IN = sum(IN_SPLITS)

kernel_name = "hybrid_mla_hgrn2_macaron_meta"


def rms_norm(x, w):
    xf = x.astype(jnp.float32)
    y = xf * lax.rsqrt(jnp.mean(xf * xf, axis=-1, keepdims=True) + EPS)
    return (y * w.astype(jnp.float32)).astype(x.dtype)


def split_cols(z, sizes):
    outs, start = [], 0
    for s in sizes:
        outs.append(z[..., start:start + s])
        start += s
    return outs


def swiglu(x, w_gu, w_down):
    gate, up = jnp.split(x @ w_gu, 2, axis=-1)
    return (jax.nn.silu(gate) * up) @ w_down


def rope(x, pos):
    half = x.shape[-1] // 2
    inv = ROPE_THETA ** (-jnp.arange(half, dtype=jnp.float32) / half)
    ang = pos.astype(jnp.float32)[:, None] * inv[None, :]
    cos = jnp.cos(ang)[:, None, :]
    sin = jnp.sin(ang)[:, None, :]
    x1 = x[..., :half].astype(jnp.float32)
    x2 = x[..., half:].astype(jnp.float32)
    return jnp.concatenate([x1 * cos - x2 * sin, x2 * cos + x1 * sin], axis=-1).astype(x.dtype)


def mla(c_q, c_kv, k_pe, pos, q_norm_w, kv_norm_w, w_uq, w_ukv):
    B, L, _ = c_q.shape
    H, DQK = MLA_HEADS, QK_NOPE + QK_ROPE
    q = (rms_norm(c_q, q_norm_w) @ w_uq).reshape(B, L, H, DQK)
    q = jnp.concatenate([q[..., :QK_NOPE], rope(q[..., QK_NOPE:], pos)], axis=-1)
    kv = (rms_norm(c_kv, kv_norm_w) @ w_ukv).reshape(B, L, H, QK_NOPE + V_HEAD)
    v = kv[..., QK_NOPE:]
    k_rot = rope(k_pe[:, :, None, :], pos)
    k = jnp.concatenate([kv[..., :QK_NOPE], jnp.broadcast_to(k_rot, (B, L, H, QK_ROPE))], axis=-1)
    scale = DQK ** -0.5
    n_blocks = -(-L // Q_BLOCK)
    pad = n_blocks * Q_BLOCK - L
    qb = jnp.pad(q, ((0, 0), (0, pad), (0, 0), (0, 0)))
    qb = qb.reshape(B, n_blocks, Q_BLOCK, H, DQK).transpose(1, 0, 2, 3, 4)
    k_pos = jnp.arange(L)

    def block(args):
        qi, blk = args
        s = jnp.einsum('bqhd,bkhd->bhqk', qi, k).astype(jnp.float32) * scale
        q_pos = blk * Q_BLOCK + jnp.arange(Q_BLOCK)
        s = jnp.where(k_pos[None, :] <= q_pos[:, None], s, NEG_BIG)
        p = jax.nn.softmax(s, axis=-1).astype(v.dtype)
        return jnp.einsum('bhqk,bkhd->bqhd', p, v)

    o = lax.map(block, (qb, jnp.arange(n_blocks)))
    o = o.transpose(1, 0, 2, 3, 4).reshape(B, n_blocks * Q_BLOCK, H * V_HEAD)
    return o[:, :L]


def hgrn2(q_in, f_in, i_in, g_in, lb, norm_w):
    B, L, _ = q_in.shape
    dt = q_in.dtype
    f32 = jnp.float32
    q = jax.nn.silu(q_in.astype(f32)).reshape(B, L, HG_HEADS, HG_KDIM)
    z = f_in.astype(f32).reshape(B, L, HG_HEADS, HG_KDIM)
    lbf = lb.astype(f32).reshape(HG_HEADS, HG_KDIM)
    f = lbf + (1.0 - lbf) * jax.nn.sigmoid(z)
    log_f = jnp.log(jnp.maximum(f, F_MIN))
    k = (1.0 - lbf) * jax.nn.sigmoid(-z)
    v = i_in.astype(f32).reshape(B, L, HG_HEADS, HG_VDIM)
    front = (-N_META) % HG_CHUNK

    def to_chunks(t):
        t = jnp.pad(t, ((0, 0), (front, 0), (0, 0), (0, 0)))
        n = t.shape[1] // HG_CHUNK
        return t.reshape(B, n, HG_CHUNK, HG_HEADS, t.shape[-1]).transpose(1, 0, 3, 2, 4)

    qc, kc, vc, gc = to_chunks(q), to_chunks(k), to_chunks(v), to_chunks(log_f)
    scale = HG_KDIM ** -0.5
    causal = jnp.tril(jnp.ones((HG_CHUNK, HG_CHUNK), dtype=bool))

    def step(S, inp):
        qt, kt, vt, gt = inp
        b = jnp.cumsum(gt, axis=2)
        diff = b[:, :, :, None, :] - b[:, :, None, :, :]
        decay = jnp.exp(jnp.where(causal[:, :, None], diff, NEG_BIG))
        attn = jnp.einsum('bhtk,bhtsk,bhsk->bhts', qt, decay, kt) * scale
        o = jnp.einsum('bhts,bhsv->bhtv', attn, vt) + \
            jnp.einsum('bhtk,bhkv->bhtv', qt * jnp.exp(b) * scale, S)
        b_last = b[:, :, -1:, :]
        S = jnp.exp(b_last[:, :, 0, :])[..., None] * S + \
            jnp.einsum('bhsk,bhsv->bhkv', kt * jnp.exp(b_last - b), vt)
        return S, o

    S0 = jnp.zeros((B, HG_HEADS, HG_KDIM, HG_VDIM), f32)
    _, o = lax.scan(step, S0, (qc, kc, vc, gc))
    n = o.shape[0]
    o = o.transpose(1, 0, 3, 2, 4).reshape(B, n * HG_CHUNK, HG_HEADS, HG_VDIM)[:, front:]
    o = o * lax.rsqrt(jnp.mean(o * o, axis=-1, keepdims=True) + EPS) * norm_w.astype(f32)
    gate = jax.nn.silu(g_in.astype(f32)).reshape(B, L, HG_HEADS, HG_VDIM)
    return (o * gate).reshape(B, L, HG_WIDTH).astype(dt)


def setup_inputs(seed: int = 0) -> dict:
    key = jax.random.key(seed)
    ks = jax.random.split(key, 24)
    f32 = jnp.float32

    def nrm(k, shape, fan_in):
        return jax.random.normal(k, shape, f32) * (fan_in ** -0.5)

    def gain(k, shape):
        return 1.0 + 0.02 * jax.random.normal(k, shape, f32)

    return {
        "x": jax.random.normal(ks[0], (BATCH, SEQ, D_MODEL), f32),
        "meta_tokens": jax.random.normal(ks[1], (N_META, D_MODEL), f32),
        "ffn1_norm": gain(ks[2], (DEPTH, D_MODEL)),
        "ffn1_w_gu": nrm(ks[3], (DEPTH, D_MODEL, 2 * D_FF), D_MODEL),
        "ffn1_w_down": nrm(ks[4], (DEPTH, D_FF, D_MODEL), D_FF),
        "mix_norm": gain(ks[5], (DEPTH, D_MODEL)),
        "w_in": nrm(ks[6], (DEPTH, D_MODEL, D_IN), D_MODEL),
        "q_norm": gain(ks[7], (DEPTH, Q_LORA)),
        "kv_norm": gain(ks[8], (DEPTH, KV_LORA)),
        "w_uq": nrm(ks[9], (DEPTH, Q_LORA, MLA_HEADS * (QK_NOPE + QK_ROPE)), Q_LORA),
        "w_ukv": nrm(ks[10], (DEPTH, KV_LORA, MLA_HEADS * (QK_NOPE + V_HEAD)), KV_LORA),
        "hg_lb_raw": 0.5 * jax.random.normal(ks[11], (DEPTH, HG_FWIDTH), f32),
        "hg_norm": gain(ks[12], (DEPTH, HG_VDIM)),
        "w_proj_attn": nrm(ks[13], (DEPTH, MLA_WIDTH, D_MODEL), MLA_WIDTH),
        "w_proj_rec": nrm(ks[14], (DEPTH, HG_WIDTH, D_MODEL), HG_WIDTH),
        "w_out": nrm(ks[15], (DEPTH, D_MODEL, D_MODEL), D_MODEL),
        "ffn2_norm": gain(ks[16], (DEPTH, D_MODEL)),
        "ffn2_w_gu": nrm(ks[17], (DEPTH, D_MODEL, 2 * D_FF), D_MODEL),
        "ffn2_w_down": nrm(ks[18], (DEPTH, D_FF, D_MODEL), D_FF),
        "final_norm": gain(ks[19], (D_MODEL,)),
    }


def reference(x, meta_tokens, ffn1_norm, ffn1_w_gu, ffn1_w_down, mix_norm, w_in, q_norm, kv_norm,
              w_uq, w_ukv, hg_lb_raw, hg_norm, w_proj_attn, w_proj_rec, w_out,
              ffn2_norm, ffn2_w_gu, ffn2_w_down, final_norm):
    B = x.shape[0]
    meta = jnp.broadcast_to(meta_tokens.astype(x.dtype)[None], (B, N_META, D_MODEL))
    h = jnp.concatenate([meta, x], axis=1)
    L = h.shape[1]
    pos = jnp.arange(L)
    p_lb = jax.nn.softmax(hg_lb_raw.astype(jnp.float32), axis=0)
    lbs = jnp.cumsum(p_lb, axis=0) - p_lb[0:1]
    for l in range(DEPTH):
        h = h + 0.5 * swiglu(rms_norm(h, ffn1_norm[l]), ffn1_w_gu[l], ffn1_w_down[l])
        u = rms_norm(h, mix_norm[l])
        c_q, c_kv, k_pe, hq, hf, hi, hg, ga, gb = split_cols(u @ w_in[l], IN_SPLITS)
        y_a = mla(c_q, c_kv, k_pe, pos, q_norm[l], kv_norm[l], w_uq[l], w_ukv[l]) @ w_proj_attn[l]
        y_b = hgrn2(hq, hf, hi, hg, lbs[l], hg_norm[l]) @ w_proj_rec[l]
        merged = jax.nn.sigmoid(ga) * y_a + jax.nn.sigmoid(gb) * y_b
        h = h + merged @ w_out[l]
        h = h + 0.5 * swiglu(rms_norm(h, ffn2_norm[l]), ffn2_w_gu[l], ffn2_w_down[l])
    return rms_norm(h, final_norm)[:, N_META:]
```

```python
import functools
import math

import jax
import jax.numpy as jnp
from jax import lax
from jax.experimental import pallas as pl
from jax.experimental.pallas import tpu as pltpu

F32 = jnp.float32
BF16 = jnp.bfloat16

N_META = 16
MLA_HEADS = 8
QK_NOPE = 64
QK_ROPE = 32
V_HEAD = 64
ROPE_THETA = 10000.0
HG_HEADS = 4
HG_DIM = 128
EPS = 1e-6
F_MIN = 1e-20
NEG_BIG = -1e30

LANES = 128
SUBLANES = 8
VMEM_LIMIT_BYTES = 56 * 1024 * 1024

SEQ_TILE = 768
MIX_TILE = 384
ROW_TILE = 512
HG_CHUNK = 64
HG_BAND = 8
FFN_CHUNKS = 2

_NT = (((1,), (1,)), ((), ()))
_TN = (((0,), (0,)), ((), ()))


def _sigmoid(x):
    return 1.0 / (1.0 + jnp.exp(-x))


def _rms(x, w):
    return x * lax.rsqrt(jnp.mean(x * x, axis=-1, keepdims=True) + EPS) * w


def _const_spec(shape):
    nd = len(shape)
    return pl.BlockSpec(shape, lambda *_: (0,) * nd, pipeline_mode=pl.Buffered(1))


def _params(sem):
    return pltpu.CompilerParams(dimension_semantics=sem, vmem_limit_bytes=VMEM_LIMIT_BYTES)


def _ffn_value(x, nw, wgu_ref, wd_ref):
    d_ff = wd_ref.shape[0]
    tf = d_ff // FFN_CHUNKS
    xn = _rms(x, nw).astype(BF16)
    acc = x
    for c in range(FFN_CHUNKS):
        g = jnp.dot(xn, wgu_ref[:, c * tf:(c + 1) * tf], preferred_element_type=F32)
        u = jnp.dot(xn, wgu_ref[:, d_ff + c * tf:d_ff + (c + 1) * tf], preferred_element_type=F32)
        a = (g * _sigmoid(g) * u).astype(BF16)
        acc = acc + jnp.dot(a, wd_ref[c * tf:(c + 1) * tf, :], preferred_element_type=F32)
    return acc


def _ffn_body(x_ref, nw_ref, wgu_ref, wd_ref, o_ref):
    o_ref[...] = _ffn_value(x_ref[...], nw_ref[...], wgu_ref, wd_ref)


def _ffn_call(h2d, nw, wgu, wd):
    rows, d = h2d.shape
    return pl.pallas_call(
        _ffn_body,
        grid=(rows // ROW_TILE,),
        in_specs=[
            pl.BlockSpec((ROW_TILE, d), lambda i: (i, 0)),
            _const_spec(nw.shape),
            _const_spec(wgu.shape),
            _const_spec(wd.shape),
        ],
        out_specs=pl.BlockSpec((ROW_TILE, d), lambda i: (i, 0)),
        out_shape=jax.ShapeDtypeStruct((rows, d), F32),
        compiler_params=_params(("parallel",)),
        name="ffn",
    )(h2d, nw, wgu, wd)


def _merge_ffn_body(h_ref, oa_ref, ob_ref, g_ref, wpa_ref, wpr_ref, wo_ref,
                    nw_ref, wgu_ref, wd_ref, fnw_ref, o_ref, *, final):
    d = h_ref.shape[1]
    ya = jnp.dot(oa_ref[...], wpa_ref[...], preferred_element_type=F32)
    yb = jnp.dot(ob_ref[...], wpr_ref[...], preferred_element_type=F32)
    merged = g_ref[:, :d].astype(F32) * ya + g_ref[:, d:].astype(F32) * yb
    h = h_ref[...] + jnp.dot(merged.astype(BF16), wo_ref[...], preferred_element_type=F32)
    h = _ffn_value(h, nw_ref[...], wgu_ref, wd_ref)
    if final:
        h = _rms(h, fnw_ref[...])
    o_ref[...] = h


def _merge_ffn_call(h2d, oa, ob, gates, wpa, wpr, wo, nw, wgu, wd, fnw, final):
    rows, d = h2d.shape
    row_spec = lambda a: pl.BlockSpec((ROW_TILE, a.shape[1]), lambda i: (i, 0))
    consts = (wpa, wpr, wo, nw, wgu, wd, fnw)
    return pl.pallas_call(
        functools.partial(_merge_ffn_body, final=final),
        grid=(rows // ROW_TILE,),
        in_specs=[row_spec(h2d), row_spec(oa), row_spec(ob), row_spec(gates)]
                 + [_const_spec(c.shape) for c in consts],
        out_specs=pl.BlockSpec((ROW_TILE, d), lambda i: (i, 0)),
        out_shape=jax.ShapeDtypeStruct((rows, d), F32),
        compiler_params=_params(("parallel",)),
        name="merge_ffn",
    )(h2d, oa, ob, gates, *consts)


def _mix_in_body(h_ref, nw_ref, wu_ref, qn_ref, kvn_ref, wuq_ref, wk_ref, wv_ref,
                 cq_ref, sq_ref, ck_ref, sk_ref,
                 q_ref, k_ref, v_ref, hh_ref, g_ref, *, q_lora, kv_lora):
    hw = MLA_HEADS * LANES
    o_kv = q_lora
    o_pe = o_kv + kv_lora
    o_hg = o_pe + 2 * LANES
    o_gt = o_hg + hh_ref.shape[2]
    u = _rms(h_ref[0], nw_ref[...]).astype(BF16)

    hh_ref[0] = jnp.dot(u, wu_ref[:, o_hg:o_gt], preferred_element_type=F32)
    g_ref[0] = _sigmoid(jnp.dot(u, wu_ref[:, o_gt:], preferred_element_type=F32)).astype(BF16)

    cq = jnp.dot(u, wu_ref[:, :o_kv], preferred_element_type=F32)
    cqn = _rms(cq, qn_ref[...]).astype(BF16)
    qf = jnp.dot(cqn, wuq_ref[:, :hw], preferred_element_type=F32)
    qs = jnp.dot(cqn, wuq_ref[:, hw:], preferred_element_type=F32)
    cq_t, sq_t = cq_ref[...], sq_ref[...]
    for hd in range(MLA_HEADS):
        sl = slice(hd * LANES, (hd + 1) * LANES)
        q_ref[0, hd] = (qf[:, sl] * cq_t + qs[:, sl] * sq_t).astype(BF16)

    ckv = jnp.dot(u, wu_ref[:, o_kv:o_pe], preferred_element_type=F32)
    ckvn = _rms(ckv, kvn_ref[...]).astype(BF16)
    kpe = jnp.dot(u, wu_ref[:, o_pe:o_hg], preferred_element_type=F32)
    krot = kpe[:, :LANES] * ck_ref[...] + kpe[:, LANES:] * sk_ref[...]
    kf = jnp.dot(ckvn, wk_ref[...], preferred_element_type=F32)
    vf = jnp.dot(ckvn, wv_ref[...], preferred_element_type=F32)
    ones_lane = (lax.broadcasted_iota(jnp.int32, (1, LANES), 1) == V_HEAD).astype(F32)
    for hd in range(MLA_HEADS):
        sl = slice(hd * LANES, (hd + 1) * LANES)
        k_ref[0, hd] = (kf[:, sl] + krot).astype(BF16)
        v_ref[0, hd] = (vf[:, sl] + ones_lane).astype(BF16)


def _mix_in_call(h3d, nw, wu, qn, kvn, wuq, wk, wv, tabs, hg_cols):
    b, lp, d = h3d.shape
    tm = MIX_TILE
    head_shape = jax.ShapeDtypeStruct((b, MLA_HEADS, lp, LANES), BF16)
    head_spec = pl.BlockSpec((1, MLA_HEADS, tm, LANES), lambda bi, i: (bi, 0, i, 0))
    tab_spec = pl.BlockSpec((tm, LANES), lambda bi, i: (i, 0))
    gate_cols = wu.shape[1] - (qn.shape[1] + kvn.shape[1] + 2 * LANES + hg_cols)
    return pl.pallas_call(
        functools.partial(_mix_in_body, q_lora=qn.shape[1], kv_lora=kvn.shape[1]),
        grid=(b, lp // tm),
        in_specs=[pl.BlockSpec((1, tm, d), lambda bi, i: (bi, i, 0))]
                 + [_const_spec(c.shape) for c in (nw, wu, qn, kvn, wuq, wk, wv)]
                 + [tab_spec] * 4,
        out_specs=[head_spec, head_spec, head_spec,
                   pl.BlockSpec((1, tm, hg_cols), lambda bi, i: (bi, i, 0)),
                   pl.BlockSpec((1, tm, gate_cols), lambda bi, i: (bi, i, 0))],
        out_shape=[head_shape, head_shape, head_shape,
                   jax.ShapeDtypeStruct((b, lp, hg_cols), F32),
                   jax.ShapeDtypeStruct((b, lp, gate_cols), BF16)],
        compiler_params=_params(("parallel", "parallel")),
        name="mix_in",
    )(h3d, nw, wu, qn, kvn, wuq, wk, wv, *tabs)


def _attn_body(q_ref, k_ref, v_ref, o_ref, acc_scr, m_scr, *, n_tiles):
    t = SEQ_TILE
    row = lax.broadcasted_iota(jnp.int32, (t, t), 0)
    col = lax.broadcasted_iota(jnp.int32, (t, t), 1)
    causal = col <= row

    def step(r0, c0, masked):
        for hh in range(2):
            q = q_ref[0, hh, pl.ds(r0, t), :]
            k = k_ref[0, hh, pl.ds(c0, t), :]
            v = v_ref[0, hh, pl.ds(c0, t), :]
            s = lax.dot_general(q, k, _NT, preferred_element_type=F32)
            if masked:
                s = jnp.where(causal, s, NEG_BIG)
            m_old = m_scr[hh]
            m_new = jnp.maximum(m_old, jnp.max(s, axis=-1, keepdims=True))
            p = jnp.exp2(s - m_new)
            alpha = jnp.exp2(m_old - m_new)
            acc_scr[hh] = alpha * acc_scr[hh] + jnp.dot(p.astype(BF16), v, preferred_element_type=F32)
            m_scr[hh] = m_new

    def q_tile(qi, carry):
        r0 = pl.multiple_of(qi * t, t)
        acc_scr[...] = jnp.zeros(acc_scr.shape, F32)
        m_scr[...] = jnp.full(m_scr.shape, NEG_BIG, F32)

        def kv_tile(kj, c):
            step(r0, pl.multiple_of(kj * t, t), False)
            return c

        lax.fori_loop(0, qi, kv_tile, 0)
        step(r0, r0, True)
        outs = []
        for hh in range(2):
            a = acc_scr[hh]
            outs.append(a[:, :V_HEAD] / a[:, V_HEAD:V_HEAD + 1])
        o_ref[0, pl.ds(r0, t), :] = jnp.concatenate(outs, axis=-1).astype(BF16)
        return carry

    lax.fori_loop(0, n_tiles, q_tile, 0)


def _attn_call(q, k, v):
    b, nh, lp, _ = q.shape
    in_spec = pl.BlockSpec((1, 2, lp, LANES), lambda bi, hp: (bi, hp, 0, 0))
    return pl.pallas_call(
        functools.partial(_attn_body, n_tiles=lp // SEQ_TILE),
        grid=(b, nh // 2),
        in_specs=[in_spec, in_spec, in_spec],
        out_specs=pl.BlockSpec((1, lp, 2 * V_HEAD), lambda bi, hp: (bi, 0, hp)),
        out_shape=jax.ShapeDtypeStruct((b, lp, nh * V_HEAD), BF16),
        scratch_shapes=[pltpu.VMEM((2, SEQ_TILE, LANES), F32),
                        pltpu.VMEM((2, SEQ_TILE, 1), F32)],
        compiler_params=_params(("parallel", "parallel")),
        name="attn",
    )(q, k, v)


def _split3_bf16(x):
    hi = x.astype(BF16)
    r1 = x - hi.astype(F32)
    mid = r1.astype(BF16)
    lo = (r1 - mid.astype(F32)).astype(BF16)
    return hi, mid, lo


def _hgrn_body(hh_ref, lbraw_ref, nw_ref, o_ref, st_scr, *, layer):
    c = HG_CHUNK
    hw = HG_HEADS * HG_DIM
    n_chunks = hh_ref.shape[1] // c

    @pl.when(pl.program_id(1) == 0)
    def _():
        st_scr[...] = jnp.zeros(st_scr.shape, F32)

    raw = lbraw_ref[...]
    depth = raw.shape[0]
    rows = [raw[j:j + 1, :] for j in range(depth)]
    mx = functools.reduce(jnp.maximum, rows)
    ex = [jnp.exp(r - mx) for r in rows]
    inv_den = 1.0 / functools.reduce(jnp.add, ex)
    lb = jnp.zeros_like(mx)
    for j in range(1, layer + 1):
        lb = lb + ex[j] * inv_den
    one_m_lb = 1.0 - lb
    nw = nw_ref[...]
    scale = HG_DIM ** -0.5

    ri = lax.broadcasted_iota(jnp.int32, (c, c), 0)
    ci = lax.broadcasted_iota(jnp.int32, (c, c), 1)
    tri = (ri >= ci).astype(BF16)
    rowc = lax.broadcasted_iota(jnp.int32, (c, 1), 0)
    ones_sq = jnp.ones((HG_DIM, HG_DIM), BF16)
    levels = []
    w = c // 2
    while w >= HG_BAND:
        levels.append((w, (rowc // w) % 2 == 1, (ri // (2 * w)) == (ci // (2 * w))))
        w //= 2
    tmod = rowc % HG_BAND

    def chunk(ic, carry):
        r0 = pl.multiple_of(ic * c, c)
        xq = hh_ref[0, pl.ds(r0, c), 0:hw]
        z = hh_ref[0, pl.ds(r0, c), hw:2 * hw]
        vin = hh_ref[0, pl.ds(r0, c), 2 * hw:3 * hw]
        gin = hh_ref[0, pl.ds(r0, c), 3 * hw:4 * hw]

        e = jnp.exp(-jnp.abs(z))
        r = 1.0 / (1.0 + e)
        er = e * r
        pos = z >= 0
        f = lb + one_m_lb * jnp.where(pos, r, er)
        logf = jnp.log(jnp.maximum(f, F_MIN))
        kk = one_m_lb * jnp.where(pos, er, r)
        qq = xq * _sigmoid(xq) * scale
        hi, mid, lo = _split3_bf16(logf)
        b_all = (jnp.dot(tri, hi, preferred_element_type=F32)
                 + jnp.dot(tri, mid, preferred_element_type=F32)
                 + jnp.dot(tri, lo, preferred_element_type=F32))

        for hd in range(HG_HEADS):
            sl = slice(hd * HG_DIM, (hd + 1) * HG_DIM)
            qh, kh, vh, bh = qq[:, sl], kk[:, sl], vin[:, sl], b_all[:, sl]

            a_off = jnp.zeros((c, c), F32)
            for w, is_q, same_parent in levels:
                ref = jnp.concatenate(
                    [jnp.broadcast_to(bh[p * 2 * w + w - 1:p * 2 * w + w, :], (2 * w, HG_DIM))
                     for p in range(c // (2 * w))], axis=0)
                qt = (qh * jnp.exp(jnp.where(is_q, bh - ref, NEG_BIG))).astype(BF16)
                kt = (kh * jnp.exp(jnp.where(is_q, NEG_BIG, ref - bh))).astype(BF16)
                a_w = lax.dot_general(qt, kt, _NT, preferred_element_type=F32)
                a_off = a_off + jnp.where(same_parent, a_w, 0.0)
            o_acc = jnp.dot(a_off.astype(BF16), vh.astype(BF16), preferred_element_type=F32)

            a0 = jnp.dot((qh * kh).astype(BF16), ones_sq, preferred_element_type=F32)
            o_acc = o_acc + a0 * vh
            for d in range(1, HG_BAND):
                k_d = pltpu.roll(kh, d, 0)
                b_d = pltpu.roll(bh, d, 0)
                v_d = pltpu.roll(vh, d, 0)
                x_d = qh * k_d * jnp.exp(jnp.where(tmod >= d, bh - b_d, NEG_BIG))
                a_d = jnp.dot(x_d.astype(BF16), ones_sq, preferred_element_type=F32)
                o_acc = o_acc + a_d * v_d

            st = st_scr[hd]
            qe = (qh * jnp.exp(bh)).astype(BF16)
            o_acc = o_acc + lax.dot_general(qe, st.astype(BF16), _NT, preferred_element_type=F32)
            b_last = bh[c - 1:c, :]
            kd = (kh * jnp.exp(b_last - bh)).astype(BF16)
            st_scr[hd] = st * jnp.exp(b_last) + lax.dot_general(
                vh.astype(BF16), kd, _TN, preferred_element_type=F32)

            gh = gin[:, sl]
            y = _rms(o_acc, nw) * (gh * _sigmoid(gh))
            o_ref[0, pl.ds(r0, c), sl] = y.astype(BF16)
        return carry

    lax.fori_loop(0, n_chunks, chunk, 0)


def _hgrn_call(hh, lb_raw, nw, layer):
    b, lp, cols = hh.shape
    hw = HG_HEADS * HG_DIM
    t = SEQ_TILE
    return pl.pallas_call(
        functools.partial(_hgrn_body, layer=layer),
        grid=(b, lp // t),
        in_specs=[pl.BlockSpec((1, t, cols), lambda bi, i: (bi, i, 0)),
                  _const_spec(lb_raw.shape), _const_spec(nw.shape)],
        out_specs=pl.BlockSpec((1, t, hw), lambda bi, i: (bi, i, 0)),
        out_shape=jax.ShapeDtypeStruct((b, lp, hw), BF16),
        scratch_shapes=[pltpu.VMEM((HG_HEADS, HG_DIM, HG_DIM), F32)],
        compiler_params=_params(("parallel", "arbitrary")),
        name="hgrn2",
    )(hh, lb_raw, nw)


def _rope_tables(lp):
    half = QK_ROPE // 2
    inv = ROPE_THETA ** (-jnp.arange(half, dtype=F32) / half)
    ang = jnp.arange(lp).astype(F32)[:, None] * inv[None, :]
    cos, sin = jnp.cos(ang), jnp.sin(ang)
    zpad = jnp.zeros((lp, LANES - QK_NOPE - QK_ROPE), F32)
    qscale = math.log2(math.e) * (QK_NOPE + QK_ROPE) ** -0.5
    cos_q = jnp.concatenate([jnp.ones((lp, QK_NOPE), F32), cos, cos, zpad], axis=1) * qscale
    sin_s = jnp.concatenate([jnp.zeros((lp, QK_NOPE), F32), -sin, sin, zpad], axis=1)
    cos_k = jnp.concatenate([jnp.zeros((lp, QK_NOPE), F32), cos, cos, zpad], axis=1)
    return cos_q, sin_s * qscale, cos_k, sin_s


def _swap_halves(w):
    half = w.shape[-1] // 2
    return jnp.concatenate([w[..., half:], w[..., :half]], axis=-1)


def _prep_mixer_weights(w_in, w_uq, w_ukv, q_lora, kv_lora):
    depth, d, _ = w_in.shape
    hg = HG_HEADS * HG_DIM
    o_pe = q_lora + kv_lora
    o_hg = o_pe + QK_ROPE
    w_pe = w_in[:, :, o_pe:o_hg]
    zl = jnp.zeros((depth, d, QK_NOPE), F32)
    zr = jnp.zeros((depth, d, LANES - QK_NOPE - QK_ROPE), F32)
    wu = jnp.concatenate([w_in[:, :, :o_pe], zl, w_pe, zr, zl, _swap_halves(w_pe), zr,
                          w_in[:, :, o_hg:]], axis=2).astype(BF16)

    dqk = QK_NOPE + QK_ROPE
    uq = w_uq.reshape(depth, q_lora, MLA_HEADS, dqk)
    zq = jnp.zeros((depth, q_lora, MLA_HEADS, LANES - dqk), F32)
    uq_plain = jnp.concatenate([uq, zq], axis=3)
    uq_swap = jnp.concatenate([jnp.zeros_like(uq[..., :QK_NOPE]), _swap_halves(uq[..., QK_NOPE:]), zq], axis=3)
    wuq = jnp.concatenate([uq_plain.reshape(depth, q_lora, -1), uq_swap.reshape(depth, q_lora, -1)],
                          axis=2).astype(BF16)

    ukv = w_ukv.reshape(depth, kv_lora, MLA_HEADS, QK_NOPE + V_HEAD)
    zk = jnp.zeros((depth, kv_lora, MLA_HEADS, LANES - QK_NOPE), F32)
    zv = jnp.zeros((depth, kv_lora, MLA_HEADS, LANES - V_HEAD), F32)
    wk = jnp.concatenate([ukv[..., :QK_NOPE], zk], axis=3).reshape(depth, kv_lora, -1).astype(BF16)
    wv = jnp.concatenate([ukv[..., QK_NOPE:], zv], axis=3).reshape(depth, kv_lora, -1).astype(BF16)
    return wu, wuq, wk, wv, 4 * hg


def kernel(x, meta_tokens, ffn1_norm, ffn1_w_gu, ffn1_w_down, mix_norm, w_in, q_norm, kv_norm,
           w_uq, w_ukv, hg_lb_raw, hg_norm, w_proj_attn, w_proj_rec, w_out,
           ffn2_norm, ffn2_w_gu, ffn2_w_down, final_norm):
    b, seq, d = x.shape
    depth = w_in.shape[0]
    q_lora, kv_lora = q_norm.shape[1], kv_norm.shape[1]
    l = N_META + seq
    lp = -(-l // SEQ_TILE) * SEQ_TILE
    assert (b * lp) % ROW_TILE == 0 and lp % MIX_TILE == 0

    meta = jnp.broadcast_to(meta_tokens.astype(x.dtype)[None], (b, N_META, d))
    h = jnp.concatenate([meta, x, jnp.zeros((b, lp - l, d), x.dtype)], axis=1)

    wgu1, wgu2 = ffn1_w_gu.astype(BF16), ffn2_w_gu.astype(BF16)
    wd1, wd2 = (0.5 * ffn1_w_down).astype(BF16), (0.5 * ffn2_w_down).astype(BF16)
    wpa, wpr, wo = w_proj_attn.astype(BF16), w_proj_rec.astype(BF16), w_out.astype(BF16)
    wu, wuq, wk, wv, hg_cols = _prep_mixer_weights(w_in, w_uq, w_ukv, q_lora, kv_lora)
    tabs = _rope_tables(lp)
    row2 = lambda a: a[:, None, :]
    n1, nm, n2, qn, kvn, hgn = map(row2, (ffn1_norm, mix_norm, ffn2_norm, q_norm, kv_norm, hg_norm))
    fnw = final_norm[None, :]
    lb_raw = hg_lb_raw.astype(F32)

    h2 = h.reshape(b * lp, d)
    for layer in range(depth):
        h2 = _ffn_call(h2, n1[layer], wgu1[layer], wd1[layer])
        q, k, v, hh, gates = _mix_in_call(h2.reshape(b, lp, d), nm[layer], wu[layer], qn[layer],
                                          kvn[layer], wuq[layer], wk[layer], wv[layer], tabs, hg_cols)
        o_attn = _attn_call(q, k, v)
        o_rec = _hgrn_call(hh, lb_raw, hgn[layer], layer)
        h2 = _merge_ffn_call(h2, o_attn.reshape(b * lp, -1), o_rec.reshape(b * lp, -1),
                             gates.reshape(b * lp, -1), wpa[layer], wpr[layer], wo[layer],
                             n2[layer], wgu2[layer], wd2[layer], fnw, final=(layer == depth - 1))
    return h2.reshape(b, lp, d)[:, N_META:l]
```

```python
import functools
import math

import jax
import jax.numpy as jnp
from jax import lax
from jax.experimental import pallas as pl
from jax.experimental.pallas import tpu as pltpu

F32 = jnp.float32
BF16 = jnp.bfloat16

N_META = 16
MLA_HEADS = 8
QK_NOPE = 64
QK_ROPE = 32
V_HEAD = 64
ROPE_THETA = 10000.0
HG_HEADS = 4
HG_DIM = 128
EPS = 1e-6
F_MIN = 1e-20
NEG_BIG = -1e30

LANES = 128
SUBLANES = 8
VMEM_LIMIT_BYTES = 56 * 1024 * 1024

SEQ_TILE = 768
Q_BLOCK = 256
KV_BLOCK = 128
VT_ROWS = 80
MIX_TILE = 384
ROW_TILE = 512
HG_CHUNK = 64
HG_BAND = 8
FFN_CHUNKS = 2

_NT = (((1,), (1,)), ((), ()))
_TN = (((0,), (0,)), ((), ()))


def _sigmoid(x):
    return 1.0 / (1.0 + jnp.exp(-x))


def _rms(x, w):
    return x * lax.rsqrt(jnp.mean(x * x, axis=-1, keepdims=True) + EPS) * w


def _const_spec(shape):
    nd = len(shape)
    return pl.BlockSpec(shape, lambda *_: (0,) * nd, pipeline_mode=pl.Buffered(1))


def _params(sem):
    return pltpu.CompilerParams(dimension_semantics=sem, vmem_limit_bytes=VMEM_LIMIT_BYTES)


def _ffn_value(x, nw, wgu_ref, wd_ref):
    d_ff = wd_ref.shape[0]
    tf = d_ff // FFN_CHUNKS
    xn = _rms(x, nw).astype(BF16)
    acc = x
    for c in range(FFN_CHUNKS):
        g = jnp.dot(xn, wgu_ref[:, c * tf:(c + 1) * tf], preferred_element_type=F32)
        u = jnp.dot(xn, wgu_ref[:, d_ff + c * tf:d_ff + (c + 1) * tf], preferred_element_type=F32)
        a = (g * _sigmoid(g) * u).astype(BF16)
        acc = acc + jnp.dot(a, wd_ref[c * tf:(c + 1) * tf, :], preferred_element_type=F32)
    return acc


def _ffn_body(x_ref, nw_ref, wgu_ref, wd_ref, o_ref):
    o_ref[...] = _ffn_value(x_ref[...], nw_ref[...], wgu_ref, wd_ref)


def _ffn_call(h2d, nw, wgu, wd):
    rows, d = h2d.shape
    return pl.pallas_call(
        _ffn_body,
        grid=(rows // ROW_TILE,),
        in_specs=[
            pl.BlockSpec((ROW_TILE, d), lambda i: (i, 0)),
            _const_spec(nw.shape),
            _const_spec(wgu.shape),
            _const_spec(wd.shape),
        ],
        out_specs=pl.BlockSpec((ROW_TILE, d), lambda i: (i, 0)),
        out_shape=jax.ShapeDtypeStruct((rows, d), F32),
        compiler_params=_params(("parallel",)),
        name="ffn",
    )(h2d, nw, wgu, wd)


def _merge_ffn_body(h_ref, oa_ref, ob_ref, g_ref, wpa_ref, wpr_ref, wo_ref,
                    nw_ref, wgu_ref, wd_ref, fnw_ref, o_ref, *, final):
    d = h_ref.shape[1]
    ya = jnp.dot(oa_ref[...], wpa_ref[...], preferred_element_type=F32)
    yb = jnp.dot(ob_ref[...], wpr_ref[...], preferred_element_type=F32)
    merged = g_ref[:, :d].astype(F32) * ya + g_ref[:, d:].astype(F32) * yb
    h = h_ref[...] + jnp.dot(merged.astype(BF16), wo_ref[...], preferred_element_type=F32)
    h = _ffn_value(h, nw_ref[...], wgu_ref, wd_ref)
    if final:
        h = _rms(h, fnw_ref[...])
    o_ref[...] = h


def _merge_ffn_call(h2d, oa, ob, gates, wpa, wpr, wo, nw, wgu, wd, fnw, final):
    rows, d = h2d.shape
    row_spec = lambda a: pl.BlockSpec((ROW_TILE, a.shape[1]), lambda i: (i, 0))
    consts = (wpa, wpr, wo, nw, wgu, wd, fnw)
    return pl.pallas_call(
        functools.partial(_merge_ffn_body, final=final),
        grid=(rows // ROW_TILE,),
        in_specs=[row_spec(h2d), row_spec(oa), row_spec(ob), row_spec(gates)]
                 + [_const_spec(c.shape) for c in consts],
        out_specs=pl.BlockSpec((ROW_TILE, d), lambda i: (i, 0)),
        out_shape=jax.ShapeDtypeStruct((rows, d), F32),
        compiler_params=_params(("parallel",)),
        name="merge_ffn",
    )(h2d, oa, ob, gates, *consts)


def _mix_in_body(h_ref, nw_ref, wu_ref, qn_ref, kvn_ref, wuq_ref, wk_ref, wvt_ref,
                 cq_ref, sq_ref, ck_ref, sk_ref,
                 q_ref, k_ref, vt_ref, hh_ref, g_ref, *, q_lora, kv_lora):
    hw = MLA_HEADS * LANES
    o_kv = q_lora
    o_pe = o_kv + kv_lora
    o_hg = o_pe + 2 * LANES
    o_gt = o_hg + hh_ref.shape[2]
    u = _rms(h_ref[0], nw_ref[...]).astype(BF16)

    hh_ref[0] = jnp.dot(u, wu_ref[:, o_hg:o_gt], preferred_element_type=F32)
    g_ref[0] = _sigmoid(jnp.dot(u, wu_ref[:, o_gt:], preferred_element_type=F32)).astype(BF16)

    cq = jnp.dot(u, wu_ref[:, :o_kv], preferred_element_type=F32)
    cqn = _rms(cq, qn_ref[...]).astype(BF16)
    qf = jnp.dot(cqn, wuq_ref[:, :hw], preferred_element_type=F32)
    qs = jnp.dot(cqn, wuq_ref[:, hw:], preferred_element_type=F32)
    cq_t, sq_t = cq_ref[...], sq_ref[...]
    for hd in range(MLA_HEADS):
        sl = slice(hd * LANES, (hd + 1) * LANES)
        q_ref[0, hd] = (qf[:, sl] * cq_t + qs[:, sl] * sq_t).astype(BF16)

    ckv = jnp.dot(u, wu_ref[:, o_kv:o_pe], preferred_element_type=F32)
    ckvn = _rms(ckv, kvn_ref[...]).astype(BF16)
    kpe = jnp.dot(u, wu_ref[:, o_pe:o_hg], preferred_element_type=F32)
    krot = kpe[:, :LANES] * ck_ref[...] + kpe[:, LANES:] * sk_ref[...]
    kf = jnp.dot(ckvn, wk_ref[...], preferred_element_type=F32)
    for hd in range(MLA_HEADS):
        sl = slice(hd * LANES, (hd + 1) * LANES)
        k_ref[0, hd] = (kf[:, sl] + krot).astype(BF16)
    vt = lax.dot_general(wvt_ref[...], ckvn, _NT, preferred_element_type=F32)
    ones_row = (lax.broadcasted_iota(jnp.int32, (vt.shape[0], 1), 0) % VT_ROWS == V_HEAD).astype(F32)
    vt = (vt + ones_row).astype(BF16)
    for hd in range(MLA_HEADS):
        for jj in range(vt.shape[1] // KV_BLOCK):
            vt_ref[0, hd, jj] = vt[hd * VT_ROWS:(hd + 1) * VT_ROWS, jj * KV_BLOCK:(jj + 1) * KV_BLOCK]


def _mix_in_call(h3d, nw, wu, qn, kvn, wuq, wk, wvt, tabs, hg_cols):
    b, lp, d = h3d.shape
    tm = MIX_TILE
    head_shape = jax.ShapeDtypeStruct((b, MLA_HEADS, lp, LANES), BF16)
    head_spec = pl.BlockSpec((1, MLA_HEADS, tm, LANES), lambda bi, i: (bi, 0, i, 0))
    tab_spec = pl.BlockSpec((tm, LANES), lambda bi, i: (i, 0))
    gate_cols = wu.shape[1] - (qn.shape[1] + kvn.shape[1] + 2 * LANES + hg_cols)
    return pl.pallas_call(
        functools.partial(_mix_in_body, q_lora=qn.shape[1], kv_lora=kvn.shape[1]),
        grid=(b, lp // tm),
        in_specs=[pl.BlockSpec((1, tm, d), lambda bi, i: (bi, i, 0))]
                 + [_const_spec(c.shape) for c in (nw, wu, qn, kvn, wuq, wk, wvt)]
                 + [tab_spec] * 4,
        out_specs=[head_spec, head_spec,
                   pl.BlockSpec((1, MLA_HEADS, tm // KV_BLOCK, VT_ROWS, KV_BLOCK),
                                lambda bi, i: (bi, 0, i, 0, 0)),
                   pl.BlockSpec((1, tm, hg_cols), lambda bi, i: (bi, i, 0)),
                   pl.BlockSpec((1, tm, gate_cols), lambda bi, i: (bi, i, 0))],
        out_shape=[head_shape, head_shape,
                   jax.ShapeDtypeStruct((b, MLA_HEADS, lp // KV_BLOCK, VT_ROWS, KV_BLOCK), BF16),
                   jax.ShapeDtypeStruct((b, lp, hg_cols), F32),
                   jax.ShapeDtypeStruct((b, lp, gate_cols), BF16)],
        compiler_params=_params(("parallel", "parallel")),
        name="mix_in",
    )(h3d, nw, wu, qn, kvn, wuq, wk, wvt, *tabs)


def _attn_body(q_ref, k_ref, vt_ref, o_ref, acc_scr, m_scr, s_scr, *, n_groups):
    n_sub = SEQ_TILE // Q_BLOCK
    n_diag = SEQ_TILE // KV_BLOCK
    key_i = lax.broadcasted_iota(jnp.int32, (KV_BLOCK, Q_BLOCK), 0)
    qry_i = lax.broadcasted_iota(jnp.int32, (KV_BLOCK, Q_BLOCK), 1)

    def needed(j, t):
        return j * KV_BLOCK < (t + 1) * Q_BLOCK

    def scores(r0, kb, slot, diag_j):
        for hh in range(2):
            k = k_ref[0, hh, pl.ds(pl.multiple_of(kb * KV_BLOCK, KV_BLOCK), KV_BLOCK), :]
            for t in range(n_sub):
                if diag_j is not None and not needed(diag_j, t):
                    if diag_j == n_diag - 1:
                        s_scr[slot, hh, t] = jnp.full((KV_BLOCK, Q_BLOCK), NEG_BIG, F32)
                    continue
                q = q_ref[0, hh, pl.ds(r0 + t * Q_BLOCK, Q_BLOCK), :]
                st = lax.dot_general(k, q, _NT, preferred_element_type=F32)
                if diag_j is not None and (diag_j + 1) * KV_BLOCK > t * Q_BLOCK + 1:
                    st = jnp.where(key_i + diag_j * KV_BLOCK <= qry_i + t * Q_BLOCK, st, NEG_BIG)
                s_scr[slot, hh, t] = st

    def update(kb, slot, diag_j):
        for hh in range(2):
            vt = vt_ref[0, hh, kb]
            for t in range(n_sub):
                if diag_j is not None and not needed(diag_j, t):
                    continue
                st = s_scr[slot, hh, t]
                m_old = m_scr[hh, t]
                m_new = jnp.maximum(m_old, jnp.max(st, axis=0, keepdims=True))
                p = jnp.exp2(st - m_new).astype(BF16)
                alpha = jnp.exp2(m_old - m_new)
                acc_scr[hh, t] = alpha * acc_scr[hh, t] + jnp.dot(vt, p, preferred_element_type=F32)
                m_scr[hh, t] = m_new

    def group(gi, carry):
        r0 = pl.multiple_of(gi * SEQ_TILE, SEQ_TILE)
        n_full = gi * n_diag
        acc_scr[...] = jnp.zeros(acc_scr.shape, F32)
        m_scr[...] = jnp.full(m_scr.shape, NEG_BIG, F32)

        scores(r0, n_full, 0, 0)
        for j in range(1, n_diag):
            scores(r0, n_full + j, j % 2, j)
            update(n_full + j - 1, (j - 1) % 2, j - 1)
        last_slot = (n_diag - 1) % 2

        def kv_pair(i, c):
            kb = 2 * i
            scores(r0, kb, 1 - last_slot, None)
            update(jnp.where(i == 0, n_full + n_diag - 1, kb - 1), last_slot, None)
            scores(r0, kb + 1, last_slot, None)
            update(kb, 1 - last_slot, None)
            return c

        lax.fori_loop(0, n_full // 2, kv_pair, 0)
        update(jnp.where(gi == 0, n_diag - 1, n_full - 1), last_slot, None)
        for t in range(n_sub):
            outs = []
            for hh in range(2):
                a = acc_scr[hh, t]
                outs.append((a[:V_HEAD, :] / a[V_HEAD:V_HEAD + 1, :]).T)
            o_ref[0, pl.ds(r0 + t * Q_BLOCK, Q_BLOCK), :] = jnp.concatenate(outs, axis=-1).astype(BF16)
        return carry

    lax.fori_loop(0, n_groups, group, 0)


def _attn_call(q, k, vt):
    b, nh, lp, _ = q.shape
    qk_spec = pl.BlockSpec((1, 2, lp, LANES), lambda bi, hp: (bi, hp, 0, 0))
    vt_spec = pl.BlockSpec((1, 2) + vt.shape[2:], lambda bi, hp: (bi, hp, 0, 0, 0))
    n_sub = SEQ_TILE // Q_BLOCK
    return pl.pallas_call(
        functools.partial(_attn_body, n_groups=lp // SEQ_TILE),
        grid=(b, nh // 2),
        in_specs=[qk_spec, qk_spec, vt_spec],
        out_specs=pl.BlockSpec((1, lp, 2 * V_HEAD), lambda bi, hp: (bi, 0, hp)),
        out_shape=jax.ShapeDtypeStruct((b, lp, nh * V_HEAD), BF16),
        scratch_shapes=[pltpu.VMEM((2, n_sub, VT_ROWS, Q_BLOCK), F32),
                        pltpu.VMEM((2, n_sub, 1, Q_BLOCK), F32),
                        pltpu.VMEM((2, 2, n_sub, KV_BLOCK, Q_BLOCK), F32)],
        compiler_params=_params(("parallel", "parallel")),
        name="attn",
    )(q, k, vt)


def _split3_bf16(x):
    hi = x.astype(BF16)
    r1 = x - hi.astype(F32)
    mid = r1.astype(BF16)
    lo = (r1 - mid.astype(F32)).astype(BF16)
    return hi, mid, lo


def _hgrn_body(hh_ref, lbraw_ref, nw_ref, o_ref, st_scr, *, layer):
    c = HG_CHUNK
    hw = HG_HEADS * HG_DIM
    n_chunks = hh_ref.shape[1] // c

    @pl.when(pl.program_id(1) == 0)
    def _():
        st_scr[...] = jnp.zeros(st_scr.shape, F32)

    raw = lbraw_ref[...]
    depth = raw.shape[0]
    rows = [raw[j:j + 1, :] for j in range(depth)]
    mx = functools.reduce(jnp.maximum, rows)
    ex = [jnp.exp(r - mx) for r in rows]
    inv_den = 1.0 / functools.reduce(jnp.add, ex)
    lb = jnp.zeros_like(mx)
    for j in range(1, layer + 1):
        lb = lb + ex[j] * inv_den
    one_m_lb = 1.0 - lb
    nw = nw_ref[...]
    scale = HG_DIM ** -0.5

    ri = lax.broadcasted_iota(jnp.int32, (c, c), 0)
    ci = lax.broadcasted_iota(jnp.int32, (c, c), 1)
    tri = (ri >= ci).astype(BF16)
    rowc = lax.broadcasted_iota(jnp.int32, (c, 1), 0)
    ones_sq = jnp.ones((HG_DIM, HG_DIM), BF16)
    levels = []
    w = c // 2
    while w >= HG_BAND:
        levels.append((w, (rowc // w) % 2 == 1, (ri // (2 * w)) == (ci // (2 * w))))
        w //= 2
    tmod = rowc % HG_BAND

    def chunk(ic, carry):
        r0 = pl.multiple_of(ic * c, c)
        xq = hh_ref[0, pl.ds(r0, c), 0:hw]
        z = hh_ref[0, pl.ds(r0, c), hw:2 * hw]
        vin = hh_ref[0, pl.ds(r0, c), 2 * hw:3 * hw]
        gin = hh_ref[0, pl.ds(r0, c), 3 * hw:4 * hw]

        e = jnp.exp(-jnp.abs(z))
        r = 1.0 / (1.0 + e)
        er = e * r
        pos = z >= 0
        f = lb + one_m_lb * jnp.where(pos, r, er)
        logf = jnp.log(jnp.maximum(f, F_MIN))
        kk = one_m_lb * jnp.where(pos, er, r)
        qq = xq * _sigmoid(xq) * scale
        hi, mid, lo = _split3_bf16(logf)
        b_all = (jnp.dot(tri, hi, preferred_element_type=F32)
                 + jnp.dot(tri, mid, preferred_element_type=F32)
                 + jnp.dot(tri, lo, preferred_element_type=F32))

        for hd in range(HG_HEADS):
            sl = slice(hd * HG_DIM, (hd + 1) * HG_DIM)
            qh, kh, vh, bh = qq[:, sl], kk[:, sl], vin[:, sl], b_all[:, sl]

            a_off = jnp.zeros((c, c), F32)
            for w, is_q, same_parent in levels:
                ref = jnp.concatenate(
                    [jnp.broadcast_to(bh[p * 2 * w + w - 1:p * 2 * w + w, :], (2 * w, HG_DIM))
                     for p in range(c // (2 * w))], axis=0)
                qt = (qh * jnp.exp(jnp.where(is_q, bh - ref, NEG_BIG))).astype(BF16)
                kt = (kh * jnp.exp(jnp.where(is_q, NEG_BIG, ref - bh))).astype(BF16)
                a_w = lax.dot_general(qt, kt, _NT, preferred_element_type=F32)
                a_off = a_off + jnp.where(same_parent, a_w, 0.0)
            o_acc = jnp.dot(a_off.astype(BF16), vh.astype(BF16), preferred_element_type=F32)

            a0 = jnp.dot((qh * kh).astype(BF16), ones_sq, preferred_element_type=F32)
            o_acc = o_acc + a0 * vh
            for d in range(1, HG_BAND):
                k_d = pltpu.roll(kh, d, 0)
                b_d = pltpu.roll(bh, d, 0)
                v_d = pltpu.roll(vh, d, 0)
                x_d = qh * k_d * jnp.exp(jnp.where(tmod >= d, bh - b_d, NEG_BIG))
                a_d = jnp.dot(x_d.astype(BF16), ones_sq, preferred_element_type=F32)
                o_acc = o_acc + a_d * v_d

            st = st_scr[hd]
            qe = (qh * jnp.exp(bh)).astype(BF16)
            o_acc = o_acc + lax.dot_general(qe, st.astype(BF16), _NT, preferred_element_type=F32)
            b_last = bh[c - 1:c, :]
            kd = (kh * jnp.exp(b_last - bh)).astype(BF16)
            st_scr[hd] = st * jnp.exp(b_last) + lax.dot_general(
                vh.astype(BF16), kd, _TN, preferred_element_type=F32)

            gh = gin[:, sl]
            y = _rms(o_acc, nw) * (gh * _sigmoid(gh))
            o_ref[0, pl.ds(r0, c), sl] = y.astype(BF16)
        return carry

    lax.fori_loop(0, n_chunks, chunk, 0)


def _hgrn_call(hh, lb_raw, nw, layer):
    b, lp, cols = hh.shape
    hw = HG_HEADS * HG_DIM
    t = SEQ_TILE
    return pl.pallas_call(
        functools.partial(_hgrn_body, layer=layer),
        grid=(b, lp // t),
        in_specs=[pl.BlockSpec((1, t, cols), lambda bi, i: (bi, i, 0)),
                  _const_spec(lb_raw.shape), _const_spec(nw.shape)],
        out_specs=pl.BlockSpec((1, t, hw), lambda bi, i: (bi, i, 0)),
        out_shape=jax.ShapeDtypeStruct((b, lp, hw), BF16),
        scratch_shapes=[pltpu.VMEM((HG_HEADS, HG_DIM, HG_DIM), F32)],
        compiler_params=_params(("parallel", "arbitrary")),
        name="hgrn2",
    )(hh, lb_raw, nw)


def _rope_tables(lp):
    half = QK_ROPE // 2
    inv = ROPE_THETA ** (-jnp.arange(half, dtype=F32) / half)
    ang = jnp.arange(lp).astype(F32)[:, None] * inv[None, :]
    cos, sin = jnp.cos(ang), jnp.sin(ang)
    zpad = jnp.zeros((lp, LANES - QK_NOPE - QK_ROPE), F32)
    qscale = math.log2(math.e) * (QK_NOPE + QK_ROPE) ** -0.5
    cos_q = jnp.concatenate([jnp.ones((lp, QK_NOPE), F32), cos, cos, zpad], axis=1) * qscale
    sin_s = jnp.concatenate([jnp.zeros((lp, QK_NOPE), F32), -sin, sin, zpad], axis=1)
    cos_k = jnp.concatenate([jnp.zeros((lp, QK_NOPE), F32), cos, cos, zpad], axis=1)
    return cos_q, sin_s * qscale, cos_k, sin_s


def _swap_halves(w):
    half = w.shape[-1] // 2
    return jnp.concatenate([w[..., half:], w[..., :half]], axis=-1)


def _prep_mixer_weights(w_in, w_uq, w_ukv, q_lora, kv_lora):
    depth, d, _ = w_in.shape
    hg = HG_HEADS * HG_DIM
    o_pe = q_lora + kv_lora
    o_hg = o_pe + QK_ROPE
    w_pe = w_in[:, :, o_pe:o_hg]
    zl = jnp.zeros((depth, d, QK_NOPE), F32)
    zr = jnp.zeros((depth, d, LANES - QK_NOPE - QK_ROPE), F32)
    wu = jnp.concatenate([w_in[:, :, :o_pe], zl, w_pe, zr, zl, _swap_halves(w_pe), zr,
                          w_in[:, :, o_hg:]], axis=2).astype(BF16)

    dqk = QK_NOPE + QK_ROPE
    uq = w_uq.reshape(depth, q_lora, MLA_HEADS, dqk)
    zq = jnp.zeros((depth, q_lora, MLA_HEADS, LANES - dqk), F32)
    uq_plain = jnp.concatenate([uq, zq], axis=3)
    uq_swap = jnp.concatenate([jnp.zeros_like(uq[..., :QK_NOPE]), _swap_halves(uq[..., QK_NOPE:]), zq], axis=3)
    wuq = jnp.concatenate([uq_plain.reshape(depth, q_lora, -1), uq_swap.reshape(depth, q_lora, -1)],
                          axis=2).astype(BF16)

    ukv = w_ukv.reshape(depth, kv_lora, MLA_HEADS, QK_NOPE + V_HEAD)
    zk = jnp.zeros((depth, kv_lora, MLA_HEADS, LANES - QK_NOPE), F32)
    zv = jnp.zeros((depth, kv_lora, MLA_HEADS, VT_ROWS - V_HEAD), F32)
    wk = jnp.concatenate([ukv[..., :QK_NOPE], zk], axis=3).reshape(depth, kv_lora, -1).astype(BF16)
    wv = jnp.concatenate([ukv[..., QK_NOPE:], zv], axis=3).reshape(depth, kv_lora, -1)
    wvt = jnp.swapaxes(wv, 1, 2).astype(BF16)
    return wu, wuq, wk, wvt, 4 * hg


def kernel(x, meta_tokens, ffn1_norm, ffn1_w_gu, ffn1_w_down, mix_norm, w_in, q_norm, kv_norm,
           w_uq, w_ukv, hg_lb_raw, hg_norm, w_proj_attn, w_proj_rec, w_out,
           ffn2_norm, ffn2_w_gu, ffn2_w_down, final_norm):
    b, seq, d = x.shape
    depth = w_in.shape[0]
    q_lora, kv_lora = q_norm.shape[1], kv_norm.shape[1]
    l = N_META + seq
    lp = -(-l // SEQ_TILE) * SEQ_TILE
    assert (b * lp) % ROW_TILE == 0 and lp % MIX_TILE == 0

    meta = jnp.broadcast_to(meta_tokens.astype(x.dtype)[None], (b, N_META, d))
    h = jnp.concatenate([meta, x, jnp.zeros((b, lp - l, d), x.dtype)], axis=1)

    wgu1, wgu2 = ffn1_w_gu.astype(BF16), ffn2_w_gu.astype(BF16)
    wd1, wd2 = (0.5 * ffn1_w_down).astype(BF16), (0.5 * ffn2_w_down).astype(BF16)
    wpa, wpr, wo = w_proj_attn.astype(BF16), w_proj_rec.astype(BF16), w_out.astype(BF16)
    wu, wuq, wk, wvt, hg_cols = _prep_mixer_weights(w_in, w_uq, w_ukv, q_lora, kv_lora)
    tabs = _rope_tables(lp)
    row2 = lambda a: a[:, None, :]
    n1, nm, n2, qn, kvn, hgn = map(row2, (ffn1_norm, mix_norm, ffn2_norm, q_norm, kv_norm, hg_norm))
    fnw = final_norm[None, :]
    lb_raw = hg_lb_raw.astype(F32)

    h2 = h.reshape(b * lp, d)
    for layer in range(depth):
        h2 = _ffn_call(h2, n1[layer], wgu1[layer], wd1[layer])
        q, k, vt, hh, gates = _mix_in_call(h2.reshape(b, lp, d), nm[layer], wu[layer], qn[layer],
                                           kvn[layer], wuq[layer], wk[layer], wvt[layer], tabs, hg_cols)
        o_attn = _attn_call(q, k, vt)
        o_rec = _hgrn_call(hh, lb_raw, hgn[layer], layer)
        h2 = _merge_ffn_call(h2, o_attn.reshape(b * lp, -1), o_rec.reshape(b * lp, -1),
                             gates.reshape(b * lp, -1), wpa[layer], wpr[layer], wo[layer],
                             n2[layer], wgu2[layer], wd2[layer], fnw, final=(layer == depth - 1))
    return h2.reshape(b, lp, d)[:, N_META:l]
```

```python
import functools
import math

import jax
import jax.numpy as jnp
from jax import lax
from jax.experimental import pallas as pl
from jax.experimental.pallas import tpu as pltpu

F32 = jnp.float32
BF16 = jnp.bfloat16

N_META = 16
MLA_HEADS = 8
QK_NOPE = 64
QK_ROPE = 32
V_HEAD = 64
ROPE_THETA = 10000.0
HG_HEADS = 4
HG_DIM = 128
EPS = 1e-6
F_MIN = 1e-20
NEG_BIG = -1e30

LANES = 128
SUBLANES = 8
VMEM_LIMIT_BYTES = 56 * 1024 * 1024

SEQ_TILE = 768
Q_BLOCK = 256
K_TILE = 256
KV_BLOCK = 128
VT_ROWS = 80
MIX_TILE = 384
ROW_TILE = 512
HG_CHUNK = 64
HG_BAND = 8
FFN_CHUNKS = 2

_NT = (((1,), (1,)), ((), ()))
_TN = (((0,), (0,)), ((), ()))


def _sigmoid(x):
    return 1.0 / (1.0 + jnp.exp(-x))


def _rms(x, w):
    return x * lax.rsqrt(jnp.mean(x * x, axis=-1, keepdims=True) + EPS) * w


def _const_spec(shape):
    nd = len(shape)
    return pl.BlockSpec(shape, lambda *_: (0,) * nd, pipeline_mode=pl.Buffered(1))


def _params(sem):
    return pltpu.CompilerParams(dimension_semantics=sem, vmem_limit_bytes=VMEM_LIMIT_BYTES)


def _ffn_value(x, nw, wgu_ref, wd_ref):
    d_ff = wd_ref.shape[0]
    tf = d_ff // FFN_CHUNKS
    xn = _rms(x, nw).astype(BF16)
    acc = x
    for c in range(FFN_CHUNKS):
        g = jnp.dot(xn, wgu_ref[:, c * tf:(c + 1) * tf], preferred_element_type=F32)
        u = jnp.dot(xn, wgu_ref[:, d_ff + c * tf:d_ff + (c + 1) * tf], preferred_element_type=F32)
        a = (g * _sigmoid(g) * u).astype(BF16)
        acc = acc + jnp.dot(a, wd_ref[c * tf:(c + 1) * tf, :], preferred_element_type=F32)
    return acc


def _ffn_body(x_ref, nw_ref, wgu_ref, wd_ref, o_ref):
    o_ref[...] = _ffn_value(x_ref[...], nw_ref[...], wgu_ref, wd_ref)


def _ffn_call(h2d, nw, wgu, wd):
    rows, d = h2d.shape
    return pl.pallas_call(
        _ffn_body,
        grid=(rows // ROW_TILE,),
        in_specs=[
            pl.BlockSpec((ROW_TILE, d), lambda i: (i, 0)),
            _const_spec(nw.shape),
            _const_spec(wgu.shape),
            _const_spec(wd.shape),
        ],
        out_specs=pl.BlockSpec((ROW_TILE, d), lambda i: (i, 0)),
        out_shape=jax.ShapeDtypeStruct((rows, d), F32),
        compiler_params=_params(("parallel",)),
        name="ffn",
    )(h2d, nw, wgu, wd)


def _merge_ffn_body(h_ref, oa_ref, ob_ref, g_ref, wpa_ref, wpr_ref, wo_ref,
                    nw_ref, wgu_ref, wd_ref, fnw_ref, o_ref, *, final):
    d = h_ref.shape[1]
    ya = jnp.dot(oa_ref[...], wpa_ref[...], preferred_element_type=F32)
    yb = jnp.dot(ob_ref[...], wpr_ref[...], preferred_element_type=F32)
    merged = g_ref[:, :d].astype(F32) * ya + g_ref[:, d:].astype(F32) * yb
    h = h_ref[...] + jnp.dot(merged.astype(BF16), wo_ref[...], preferred_element_type=F32)
    h = _ffn_value(h, nw_ref[...], wgu_ref, wd_ref)
    if final:
        h = _rms(h, fnw_ref[...])
    o_ref[...] = h


def _merge_ffn_call(h2d, oa, ob, gates, wpa, wpr, wo, nw, wgu, wd, fnw, final):
    rows, d = h2d.shape
    row_spec = lambda a: pl.BlockSpec((ROW_TILE, a.shape[1]), lambda i: (i, 0))
    consts = (wpa, wpr, wo, nw, wgu, wd, fnw)
    return pl.pallas_call(
        functools.partial(_merge_ffn_body, final=final),
        grid=(rows // ROW_TILE,),
        in_specs=[row_spec(h2d), row_spec(oa), row_spec(ob), row_spec(gates)]
                 + [_const_spec(c.shape) for c in consts],
        out_specs=pl.BlockSpec((ROW_TILE, d), lambda i: (i, 0)),
        out_shape=jax.ShapeDtypeStruct((rows, d), F32),
        compiler_params=_params(("parallel",)),
        name="merge_ffn",
    )(h2d, oa, ob, gates, *consts)


def _mix_in_body(h_ref, nw_ref, wu_ref, qn_ref, kvn_ref, wuq_ref, wk_ref, wvt_ref,
                 cq_ref, sq_ref, ck_ref, sk_ref,
                 q_ref, k_ref, vt_ref, hh_ref, g_ref, *, q_lora, kv_lora):
    hw = MLA_HEADS * LANES
    o_kv = q_lora
    o_pe = o_kv + kv_lora
    o_hg = o_pe + 2 * LANES
    o_gt = o_hg + hh_ref.shape[2]
    u = _rms(h_ref[0], nw_ref[...]).astype(BF16)

    hh_ref[0] = jnp.dot(u, wu_ref[:, o_hg:o_gt], preferred_element_type=F32)
    g_ref[0] = _sigmoid(jnp.dot(u, wu_ref[:, o_gt:], preferred_element_type=F32)).astype(BF16)

    cq = jnp.dot(u, wu_ref[:, :o_kv], preferred_element_type=F32)
    cqn = _rms(cq, qn_ref[...]).astype(BF16)
    qf = jnp.dot(cqn, wuq_ref[:, :hw], preferred_element_type=F32)
    qs = jnp.dot(cqn, wuq_ref[:, hw:], preferred_element_type=F32)
    cq_t, sq_t = cq_ref[...], sq_ref[...]
    for hd in range(MLA_HEADS):
        sl = slice(hd * LANES, (hd + 1) * LANES)
        q_ref[0, hd] = (qf[:, sl] * cq_t + qs[:, sl] * sq_t).astype(BF16)

    ckv = jnp.dot(u, wu_ref[:, o_kv:o_pe], preferred_element_type=F32)
    ckvn = _rms(ckv, kvn_ref[...]).astype(BF16)
    kpe = jnp.dot(u, wu_ref[:, o_pe:o_hg], preferred_element_type=F32)
    krot = kpe[:, :LANES] * ck_ref[...] + kpe[:, LANES:] * sk_ref[...]
    kf = jnp.dot(ckvn, wk_ref[...], preferred_element_type=F32)
    for hd in range(MLA_HEADS):
        sl = slice(hd * LANES, (hd + 1) * LANES)
        k_ref[0, hd] = (kf[:, sl] + krot).astype(BF16)
    vt = lax.dot_general(wvt_ref[...], ckvn, _NT, preferred_element_type=F32)
    ones_row = (lax.broadcasted_iota(jnp.int32, (vt.shape[0], 1), 0) % VT_ROWS == V_HEAD).astype(F32)
    vt = (vt + ones_row).astype(BF16)
    for hd in range(MLA_HEADS):
        for jj in range(vt.shape[1] // KV_BLOCK):
            vt_ref[0, hd, jj] = vt[hd * VT_ROWS:(hd + 1) * VT_ROWS, jj * KV_BLOCK:(jj + 1) * KV_BLOCK]


def _mix_in_call(h3d, nw, wu, qn, kvn, wuq, wk, wvt, tabs, hg_cols):
    b, lp, d = h3d.shape
    tm = MIX_TILE
    head_shape = jax.ShapeDtypeStruct((b, MLA_HEADS, lp, LANES), BF16)
    head_spec = pl.BlockSpec((1, MLA_HEADS, tm, LANES), lambda bi, i: (bi, 0, i, 0))
    tab_spec = pl.BlockSpec((tm, LANES), lambda bi, i: (i, 0))
    gate_cols = wu.shape[1] - (qn.shape[1] + kvn.shape[1] + 2 * LANES + hg_cols)
    return pl.pallas_call(
        functools.partial(_mix_in_body, q_lora=qn.shape[1], kv_lora=kvn.shape[1]),
        grid=(b, lp // tm),
        in_specs=[pl.BlockSpec((1, tm, d), lambda bi, i: (bi, i, 0))]
                 + [_const_spec(c.shape) for c in (nw, wu, qn, kvn, wuq, wk, wvt)]
                 + [tab_spec] * 4,
        out_specs=[head_spec, head_spec,
                   pl.BlockSpec((1, MLA_HEADS, tm // KV_BLOCK, VT_ROWS, KV_BLOCK),
                                lambda bi, i: (bi, 0, i, 0, 0)),
                   pl.BlockSpec((1, tm, hg_cols), lambda bi, i: (bi, i, 0)),
                   pl.BlockSpec((1, tm, gate_cols), lambda bi, i: (bi, i, 0))],
        out_shape=[head_shape, head_shape,
                   jax.ShapeDtypeStruct((b, MLA_HEADS, lp // KV_BLOCK, VT_ROWS, KV_BLOCK), BF16),
                   jax.ShapeDtypeStruct((b, lp, hg_cols), F32),
                   jax.ShapeDtypeStruct((b, lp, gate_cols), BF16)],
        compiler_params=_params(("parallel", "parallel")),
        name="mix_in",
    )(h3d, nw, wu, qn, kvn, wuq, wk, wvt, *tabs)


def _attn_body(q_ref, k_ref, vt_ref, o_ref, acc_scr, m_scr, s_scr, qt_scr, *, n_groups):
    n_sub = SEQ_TILE // Q_BLOCK
    n_diag = SEQ_TILE // K_TILE
    vt_per_tile = K_TILE // KV_BLOCK
    key_i = lax.broadcasted_iota(jnp.int32, (K_TILE, Q_BLOCK), 0)
    qry_i = lax.broadcasted_iota(jnp.int32, (K_TILE, Q_BLOCK), 1)

    def needed(j, t):
        return j * K_TILE < (t + 1) * Q_BLOCK

    def scores(kb, slot, diag_j):
        for hh in range(2):
            k = k_ref[0, hh, pl.ds(pl.multiple_of(kb * K_TILE, K_TILE), K_TILE), :]
            for t in range(n_sub):
                if diag_j is not None and not needed(diag_j, t):
                    if diag_j == n_diag - 1:
                        s_scr[slot, hh, t] = jnp.full((K_TILE, Q_BLOCK), NEG_BIG, F32)
                    continue
                st = jnp.dot(k, qt_scr[hh, :, t * Q_BLOCK:(t + 1) * Q_BLOCK],
                             preferred_element_type=F32)
                if diag_j is not None and (diag_j + 1) * K_TILE > t * Q_BLOCK + 1:
                    st = jnp.where(key_i + diag_j * K_TILE <= qry_i + t * Q_BLOCK, st, NEG_BIG)
                s_scr[slot, hh, t] = st

    def update(kb, slot, diag_j):
        for hh in range(2):
            vt = jnp.concatenate([vt_ref[0, hh, kb * vt_per_tile + i] for i in range(vt_per_tile)], axis=1)
            for t in range(n_sub):
                if diag_j is not None and not needed(diag_j, t):
                    continue
                m_old = m_scr[hh, t]
                m_new = jnp.maximum(m_old, jnp.max(s_scr[slot, hh, t], axis=0, keepdims=True))
                p = jnp.exp2(s_scr[slot, hh, t] - m_new).astype(BF16)
                alpha = jnp.exp2(m_old - m_new)
                acc_scr[hh, t] = alpha * acc_scr[hh, t] + jnp.dot(vt, p, preferred_element_type=F32)
                m_scr[hh, t] = m_new

    def group(gi, carry):
        r0 = pl.multiple_of(gi * SEQ_TILE, SEQ_TILE)
        n_full = gi * n_diag
        acc_scr[...] = jnp.zeros(acc_scr.shape, F32)
        m_scr[...] = jnp.full(m_scr.shape, NEG_BIG, F32)
        for hh in range(2):
            for t in range(n_sub):
                qb = q_ref[0, hh, pl.ds(r0 + t * Q_BLOCK, Q_BLOCK), :]
                qt_scr[hh, :, t * Q_BLOCK:(t + 1) * Q_BLOCK] = qb.astype(F32).T.astype(BF16)

        scores(n_full, 0, 0)
        for j in range(1, n_diag):
            scores(n_full + j, j, j)
            update(n_full + j - 1, j - 1, j - 1)

        def kv_trip(i, c):
            kb = i * n_diag
            scores(kb, 0, None)
            update(jnp.where(i == 0, n_full + n_diag - 1, kb - 1), n_diag - 1, None)
            for j in range(1, n_diag):
                scores(kb + j, j, None)
                update(kb + j - 1, j - 1, None)
            return c

        lax.fori_loop(0, gi, kv_trip, 0)
        update(jnp.where(gi == 0, n_diag - 1, n_full - 1), n_diag - 1, None)
        for t in range(n_sub):
            outs = []
            for hh in range(2):
                a = acc_scr[hh, t]
                outs.append((a[:V_HEAD, :] / a[V_HEAD:V_HEAD + 1, :]).T)
            o_ref[0, pl.ds(r0 + t * Q_BLOCK, Q_BLOCK), :] = jnp.concatenate(outs, axis=-1).astype(BF16)
        return carry

    lax.fori_loop(0, n_groups, group, 0)


def _attn_call(q, k, vt):
    b, nh, lp, _ = q.shape
    qk_spec = pl.BlockSpec((1, 2, lp, LANES), lambda bi, hp: (bi, hp, 0, 0))
    vt_spec = pl.BlockSpec((1, 2) + vt.shape[2:], lambda bi, hp: (bi, hp, 0, 0, 0))
    n_sub = SEQ_TILE // Q_BLOCK
    return pl.pallas_call(
        functools.partial(_attn_body, n_groups=lp // SEQ_TILE),
        grid=(b, nh // 2),
        in_specs=[qk_spec, qk_spec, vt_spec],
        out_specs=pl.BlockSpec((1, lp, 2 * V_HEAD), lambda bi, hp: (bi, 0, hp)),
        out_shape=jax.ShapeDtypeStruct((b, lp, nh * V_HEAD), BF16),
        scratch_shapes=[pltpu.VMEM((2, n_sub, VT_ROWS, Q_BLOCK), F32),
                        pltpu.VMEM((2, n_sub, 1, Q_BLOCK), F32),
                        pltpu.VMEM((SEQ_TILE // K_TILE, 2, n_sub, K_TILE, Q_BLOCK), F32),
                        pltpu.VMEM((2, LANES, SEQ_TILE), BF16)],
        compiler_params=_params(("parallel", "parallel")),
        name="attn",
    )(q, k, vt)


def _split3_bf16(x):
    hi = x.astype(BF16)
    r1 = x - hi.astype(F32)
    mid = r1.astype(BF16)
    lo = (r1 - mid.astype(F32)).astype(BF16)
    return hi, mid, lo


def _hgrn_body(hh_ref, lbraw_ref, nw_ref, o_ref, st_scr, *, layer):
    c = HG_CHUNK
    hw = HG_HEADS * HG_DIM
    n_chunks = hh_ref.shape[1] // c

    @pl.when(pl.program_id(1) == 0)
    def _():
        st_scr[...] = jnp.zeros(st_scr.shape, F32)

    raw = lbraw_ref[...]
    depth = raw.shape[0]
    rows = [raw[j:j + 1, :] for j in range(depth)]
    mx = functools.reduce(jnp.maximum, rows)
    ex = [jnp.exp(r - mx) for r in rows]
    inv_den = 1.0 / functools.reduce(jnp.add, ex)
    lb = jnp.zeros_like(mx)
    for j in range(1, layer + 1):
        lb = lb + ex[j] * inv_den
    one_m_lb = 1.0 - lb
    nw = nw_ref[...]
    scale = HG_DIM ** -0.5

    ri = lax.broadcasted_iota(jnp.int32, (c, c), 0)
    ci = lax.broadcasted_iota(jnp.int32, (c, c), 1)
    tri = (ri >= ci).astype(BF16)
    rowc = lax.broadcasted_iota(jnp.int32, (c, 1), 0)
    ones_sq = jnp.ones((HG_DIM, HG_DIM), BF16)
    levels = []
    w = c // 2
    while w >= HG_BAND:
        levels.append((w, (rowc // w) % 2 == 1, (ri // (2 * w)) == (ci // (2 * w))))
        w //= 2
    tmod = rowc % HG_BAND

    def chunk(ic, carry):
        r0 = pl.multiple_of(ic * c, c)
        xq = hh_ref[0, pl.ds(r0, c), 0:hw]
        z = hh_ref[0, pl.ds(r0, c), hw:2 * hw]
        vin = hh_ref[0, pl.ds(r0, c), 2 * hw:3 * hw]
        gin = hh_ref[0, pl.ds(r0, c), 3 * hw:4 * hw]

        e = jnp.exp(-jnp.abs(z))
        r = 1.0 / (1.0 + e)
        er = e * r
        pos = z >= 0
        f = lb + one_m_lb * jnp.where(pos, r, er)
        logf = jnp.log(jnp.maximum(f, F_MIN))
        kk = one_m_lb * jnp.where(pos, er, r)
        qq = xq * _sigmoid(xq) * scale
        hi, mid, lo = _split3_bf16(logf)
        b_all = (jnp.dot(tri, hi, preferred_element_type=F32)
                 + jnp.dot(tri, mid, preferred_element_type=F32)
                 + jnp.dot(tri, lo, preferred_element_type=F32))

        for hd in range(HG_HEADS):
            sl = slice(hd * HG_DIM, (hd + 1) * HG_DIM)
            qh, kh, vh, bh = qq[:, sl], kk[:, sl], vin[:, sl], b_all[:, sl]

            a_off = jnp.zeros((c, c), F32)
            for w, is_q, same_parent in levels:
                ref = jnp.concatenate(
                    [jnp.broadcast_to(bh[p * 2 * w + w - 1:p * 2 * w + w, :], (2 * w, HG_DIM))
                     for p in range(c // (2 * w))], axis=0)
                qt = (qh * jnp.exp(jnp.where(is_q, bh - ref, NEG_BIG))).astype(BF16)
                kt = (kh * jnp.exp(jnp.where(is_q, NEG_BIG, ref - bh))).astype(BF16)
                a_w = lax.dot_general(qt, kt, _NT, preferred_element_type=F32)
                a_off = a_off + jnp.where(same_parent, a_w, 0.0)
            o_acc = jnp.dot(a_off.astype(BF16), vh.astype(BF16), preferred_element_type=F32)

            a0 = jnp.dot((qh * kh).astype(BF16), ones_sq, preferred_element_type=F32)
            o_acc = o_acc + a0 * vh
            for d in range(1, HG_BAND):
                k_d = pltpu.roll(kh, d, 0)
                b_d = pltpu.roll(bh, d, 0)
                v_d = pltpu.roll(vh, d, 0)
                x_d = qh * k_d * jnp.exp(jnp.where(tmod >= d, bh - b_d, NEG_BIG))
                a_d = jnp.dot(x_d.astype(BF16), ones_sq, preferred_element_type=F32)
                o_acc = o_acc + a_d * v_d

            st = st_scr[hd]
            qe = (qh * jnp.exp(bh)).astype(BF16)
            o_acc = o_acc + lax.dot_general(qe, st.astype(BF16), _NT, preferred_element_type=F32)
            b_last = bh[c - 1:c, :]
            kd = (kh * jnp.exp(b_last - bh)).astype(BF16)
            st_scr[hd] = st * jnp.exp(b_last) + lax.dot_general(
                vh.astype(BF16), kd, _TN, preferred_element_type=F32)

            gh = gin[:, sl]
            y = _rms(o_acc, nw) * (gh * _sigmoid(gh))
            o_ref[0, pl.ds(r0, c), sl] = y.astype(BF16)
        return carry

    lax.fori_loop(0, n_chunks, chunk, 0)


def _hgrn_call(hh, lb_raw, nw, layer):
    b, lp, cols = hh.shape
    hw = HG_HEADS * HG_DIM
    t = SEQ_TILE
    return pl.pallas_call(
        functools.partial(_hgrn_body, layer=layer),
        grid=(b, lp // t),
        in_specs=[pl.BlockSpec((1, t, cols), lambda bi, i: (bi, i, 0)),
                  _const_spec(lb_raw.shape), _const_spec(nw.shape)],
        out_specs=pl.BlockSpec((1, t, hw), lambda bi, i: (bi, i, 0)),
        out_shape=jax.ShapeDtypeStruct((b, lp, hw), BF16),
        scratch_shapes=[pltpu.VMEM((HG_HEADS, HG_DIM, HG_DIM), F32)],
        compiler_params=_params(("parallel", "arbitrary")),
        name="hgrn2",
    )(hh, lb_raw, nw)


def _rope_tables(lp):
    half = QK_ROPE // 2
    inv = ROPE_THETA ** (-jnp.arange(half, dtype=F32) / half)
    ang = jnp.arange(lp).astype(F32)[:, None] * inv[None, :]
    cos, sin = jnp.cos(ang), jnp.sin(ang)
    zpad = jnp.zeros((lp, LANES - QK_NOPE - QK_ROPE), F32)
    qscale = math.log2(math.e) * (QK_NOPE + QK_ROPE) ** -0.5
    cos_q = jnp.concatenate([jnp.ones((lp, QK_NOPE), F32), cos, cos, zpad], axis=1) * qscale
    sin_s = jnp.concatenate([jnp.zeros((lp, QK_NOPE), F32), -sin, sin, zpad], axis=1)
    cos_k = jnp.concatenate([jnp.zeros((lp, QK_NOPE), F32), cos, cos, zpad], axis=1)
    return cos_q, sin_s * qscale, cos_k, sin_s


def _swap_halves(w):
    half = w.shape[-1] // 2
    return jnp.concatenate([w[..., half:], w[..., :half]], axis=-1)


def _prep_mixer_weights(w_in, w_uq, w_ukv, q_lora, kv_lora):
    depth, d, _ = w_in.shape
    hg = HG_HEADS * HG_DIM
    o_pe = q_lora + kv_lora
    o_hg = o_pe + QK_ROPE
    w_pe = w_in[:, :, o_pe:o_hg]
    zl = jnp.zeros((depth, d, QK_NOPE), F32)
    zr = jnp.zeros((depth, d, LANES - QK_NOPE - QK_ROPE), F32)
    wu = jnp.concatenate([w_in[:, :, :o_pe], zl, w_pe, zr, zl, _swap_halves(w_pe), zr,
                          w_in[:, :, o_hg:]], axis=2).astype(BF16)

    dqk = QK_NOPE + QK_ROPE
    uq = w_uq.reshape(depth, q_lora, MLA_HEADS, dqk)
    zq = jnp.zeros((depth, q_lora, MLA_HEADS, LANES - dqk), F32)
    uq_plain = jnp.concatenate([uq, zq], axis=3)
    uq_swap = jnp.concatenate([jnp.zeros_like(uq[..., :QK_NOPE]), _swap_halves(uq[..., QK_NOPE:]), zq], axis=3)
    wuq = jnp.concatenate([uq_plain.reshape(depth, q_lora, -1), uq_swap.reshape(depth, q_lora, -1)],
                          axis=2).astype(BF16)

    ukv = w_ukv.reshape(depth, kv_lora, MLA_HEADS, QK_NOPE + V_HEAD)
    zk = jnp.zeros((depth, kv_lora, MLA_HEADS, LANES - QK_NOPE), F32)
    zv = jnp.zeros((depth, kv_lora, MLA_HEADS, VT_ROWS - V_HEAD), F32)
    wk = jnp.concatenate([ukv[..., :QK_NOPE], zk], axis=3).reshape(depth, kv_lora, -1).astype(BF16)
    wv = jnp.concatenate([ukv[..., QK_NOPE:], zv], axis=3).reshape(depth, kv_lora, -1)
    wvt = jnp.swapaxes(wv, 1, 2).astype(BF16)
    return wu, wuq, wk, wvt, 4 * hg


def kernel(x, meta_tokens, ffn1_norm, ffn1_w_gu, ffn1_w_down, mix_norm, w_in, q_norm, kv_norm,
           w_uq, w_ukv, hg_lb_raw, hg_norm, w_proj_attn, w_proj_rec, w_out,
           ffn2_norm, ffn2_w_gu, ffn2_w_down, final_norm):
    b, seq, d = x.shape
    depth = w_in.shape[0]
    q_lora, kv_lora = q_norm.shape[1], kv_norm.shape[1]
    l = N_META + seq
    lp = -(-l // SEQ_TILE) * SEQ_TILE
    assert (b * lp) % ROW_TILE == 0 and lp % MIX_TILE == 0

    meta = jnp.broadcast_to(meta_tokens.astype(x.dtype)[None], (b, N_META, d))
    h = jnp.concatenate([meta, x, jnp.zeros((b, lp - l, d), x.dtype)], axis=1)

    wgu1, wgu2 = ffn1_w_gu.astype(BF16), ffn2_w_gu.astype(BF16)
    wd1, wd2 = (0.5 * ffn1_w_down).astype(BF16), (0.5 * ffn2_w_down).astype(BF16)
    wpa, wpr, wo = w_proj_attn.astype(BF16), w_proj_rec.astype(BF16), w_out.astype(BF16)
    wu, wuq, wk, wvt, hg_cols = _prep_mixer_weights(w_in, w_uq, w_ukv, q_lora, kv_lora)
    tabs = _rope_tables(lp)
    row2 = lambda a: a[:, None, :]
    n1, nm, n2, qn, kvn, hgn = map(row2, (ffn1_norm, mix_norm, ffn2_norm, q_norm, kv_norm, hg_norm))
    fnw = final_norm[None, :]
    lb_raw = hg_lb_raw.astype(F32)

    h2 = h.reshape(b * lp, d)
    for layer in range(depth):
        h2 = _ffn_call(h2, n1[layer], wgu1[layer], wd1[layer])
        q, k, vt, hh, gates = _mix_in_call(h2.reshape(b, lp, d), nm[layer], wu[layer], qn[layer],
                                           kvn[layer], wuq[layer], wk[layer], wvt[layer], tabs, hg_cols)
        o_attn = _attn_call(q, k, vt)
        o_rec = _hgrn_call(hh, lb_raw, hgn[layer], layer)
        h2 = _merge_ffn_call(h2, o_attn.reshape(b * lp, -1), o_rec.reshape(b * lp, -1),
                             gates.reshape(b * lp, -1), wpa[layer], wpr[layer], wo[layer],
                             n2[layer], wgu2[layer], wd2[layer], fnw, final=(layer == depth - 1))
    return h2.reshape(b, lp, d)[:, N_META:l]
```

```python
import functools
import math

import numpy as np
import jax
import jax.numpy as jnp
from jax import lax
from jax.experimental import pallas as pl
from jax.experimental.pallas import tpu as pltpu

F32 = jnp.float32
BF16 = jnp.bfloat16

N_META = 16
MLA_HEADS = 8
QK_NOPE = 64
QK_ROPE = 32
V_HEAD = 64
ROPE_THETA = 10000.0
HG_HEADS = 4
HG_DIM = 128
EPS = 1e-6
F_MIN = 1e-20
NEG_BIG = -1e30
LOG2E = math.log2(math.e)

LANES = 128
SUBLANES = 8
VMEM_LIMIT_BYTES = 56 * 1024 * 1024

SEQ_TILE = 768
Q_BLOCK = 256
K_TILE = 256
KV_BLOCK = 128
VT_ROWS = 80
MIX_TILE = 384
ROW_TILE = 512
HG_CHUNK = 64
HG_UNROLL = 2
FFN_CHUNKS = 2

_NT = (((1,), (1,)), ((), ()))
_TN = (((0,), (0,)), ((), ()))


def _sigmoid(x):
    return 1.0 / (1.0 + jnp.exp(-x))


def _rms(x, w):
    return x * lax.rsqrt(jnp.mean(x * x, axis=-1, keepdims=True) + EPS) * w


def _const_spec(shape):
    nd = len(shape)
    return pl.BlockSpec(shape, lambda *_: (0,) * nd, pipeline_mode=pl.Buffered(1))


def _params(sem):
    return pltpu.CompilerParams(dimension_semantics=sem, vmem_limit_bytes=VMEM_LIMIT_BYTES)


def _ffn_value(x, nw, wgu_ref, wd_ref):
    d_ff = wd_ref.shape[0]
    tf = d_ff // FFN_CHUNKS
    xn = _rms(x, nw).astype(BF16)
    acc = x
    for c in range(FFN_CHUNKS):
        g = jnp.dot(xn, wgu_ref[:, c * tf:(c + 1) * tf], preferred_element_type=F32)
        u = jnp.dot(xn, wgu_ref[:, d_ff + c * tf:d_ff + (c + 1) * tf], preferred_element_type=F32)
        a = (g * _sigmoid(g) * u).astype(BF16)
        acc = acc + jnp.dot(a, wd_ref[c * tf:(c + 1) * tf, :], preferred_element_type=F32)
    return acc


def _ffn_body(x_ref, nw_ref, wgu_ref, wd_ref, o_ref):
    o_ref[...] = _ffn_value(x_ref[...], nw_ref[...], wgu_ref, wd_ref)


def _ffn_call(h2d, nw, wgu, wd):
    rows, d = h2d.shape
    return pl.pallas_call(
        _ffn_body,
        grid=(rows // ROW_TILE,),
        in_specs=[
            pl.BlockSpec((ROW_TILE, d), lambda i: (i, 0)),
            _const_spec(nw.shape),
            _const_spec(wgu.shape),
            _const_spec(wd.shape),
        ],
        out_specs=pl.BlockSpec((ROW_TILE, d), lambda i: (i, 0)),
        out_shape=jax.ShapeDtypeStruct((rows, d), F32),
        compiler_params=_params(("parallel",)),
        name="ffn",
    )(h2d, nw, wgu, wd)


def _merge_ffn_body(h_ref, oa_ref, ob_ref, g_ref, wpa_ref, wpr_ref, wo_ref,
                    nw_ref, wgu_ref, wd_ref, fnw_ref, o_ref, *, final):
    d = h_ref.shape[1]
    ya = jnp.dot(oa_ref[...], wpa_ref[...], preferred_element_type=F32)
    yb = jnp.dot(ob_ref[...], wpr_ref[...], preferred_element_type=F32)
    merged = g_ref[:, :d].astype(F32) * ya + g_ref[:, d:].astype(F32) * yb
    h = h_ref[...] + jnp.dot(merged.astype(BF16), wo_ref[...], preferred_element_type=F32)
    h = _ffn_value(h, nw_ref[...], wgu_ref, wd_ref)
    if final:
        h = _rms(h, fnw_ref[...])
    o_ref[...] = h


def _merge_ffn_call(h2d, oa, ob, gates, wpa, wpr, wo, nw, wgu, wd, fnw, final):
    rows, d = h2d.shape
    row_spec = lambda a: pl.BlockSpec((ROW_TILE, a.shape[1]), lambda i: (i, 0))
    consts = (wpa, wpr, wo, nw, wgu, wd, fnw)
    return pl.pallas_call(
        functools.partial(_merge_ffn_body, final=final),
        grid=(rows // ROW_TILE,),
        in_specs=[row_spec(h2d), row_spec(oa), row_spec(ob), row_spec(gates)]
                 + [_const_spec(c.shape) for c in consts],
        out_specs=pl.BlockSpec((ROW_TILE, d), lambda i: (i, 0)),
        out_shape=jax.ShapeDtypeStruct((rows, d), F32),
        compiler_params=_params(("parallel",)),
        name="merge_ffn",
    )(h2d, oa, ob, gates, *consts)


def _mix_in_body(h_ref, nw_ref, wu_ref, qn_ref, kvn_ref, wuq_ref, wk_ref, wvt_ref,
                 cq_ref, sq_ref, ck_ref, sk_ref,
                 q_ref, k_ref, vt_ref, hh_ref, g_ref, *, q_lora, kv_lora):
    hw = MLA_HEADS * LANES
    o_kv = q_lora
    o_pe = o_kv + kv_lora
    o_hg = o_pe + 2 * LANES
    o_gt = o_hg + hh_ref.shape[2]
    u = _rms(h_ref[0], nw_ref[...]).astype(BF16)

    hh_ref[0] = jnp.dot(u, wu_ref[:, o_hg:o_gt], preferred_element_type=F32)
    g_ref[0] = _sigmoid(jnp.dot(u, wu_ref[:, o_gt:], preferred_element_type=F32)).astype(BF16)

    cq = jnp.dot(u, wu_ref[:, :o_kv], preferred_element_type=F32)
    cqn = _rms(cq, qn_ref[...]).astype(BF16)
    qf = jnp.dot(cqn, wuq_ref[:, :hw], preferred_element_type=F32)
    qs = jnp.dot(cqn, wuq_ref[:, hw:], preferred_element_type=F32)
    cq_t, sq_t = cq_ref[...], sq_ref[...]
    for hd in range(MLA_HEADS):
        sl = slice(hd * LANES, (hd + 1) * LANES)
        q_ref[0, hd] = (qf[:, sl] * cq_t + qs[:, sl] * sq_t).astype(BF16)

    ckv = jnp.dot(u, wu_ref[:, o_kv:o_pe], preferred_element_type=F32)
    ckvn = _rms(ckv, kvn_ref[...]).astype(BF16)
    kpe = jnp.dot(u, wu_ref[:, o_pe:o_hg], preferred_element_type=F32)
    krot = kpe[:, :LANES] * ck_ref[...] + kpe[:, LANES:] * sk_ref[...]
    kf = jnp.dot(ckvn, wk_ref[...], preferred_element_type=F32)
    for hd in range(MLA_HEADS):
        sl = slice(hd * LANES, (hd + 1) * LANES)
        k_ref[0, hd] = (kf[:, sl] + krot).astype(BF16)
    vt = lax.dot_general(wvt_ref[...], ckvn, _NT, preferred_element_type=F32)
    ones_row = (lax.broadcasted_iota(jnp.int32, (vt.shape[0], 1), 0) % VT_ROWS == V_HEAD).astype(F32)
    vt = (vt + ones_row).astype(BF16)
    for hd in range(MLA_HEADS):
        for jj in range(vt.shape[1] // KV_BLOCK):
            vt_ref[0, hd, jj] = vt[hd * VT_ROWS:(hd + 1) * VT_ROWS, jj * KV_BLOCK:(jj + 1) * KV_BLOCK]


def _mix_in_call(h3d, nw, wu, qn, kvn, wuq, wk, wvt, tabs, hg_cols):
    b, lp, d = h3d.shape
    tm = MIX_TILE
    head_shape = jax.ShapeDtypeStruct((b, MLA_HEADS, lp, LANES), BF16)
    head_spec = pl.BlockSpec((1, MLA_HEADS, tm, LANES), lambda bi, i: (bi, 0, i, 0))
    tab_spec = pl.BlockSpec((tm, LANES), lambda bi, i: (i, 0))
    gate_cols = wu.shape[1] - (qn.shape[1] + kvn.shape[1] + 2 * LANES + hg_cols)
    return pl.pallas_call(
        functools.partial(_mix_in_body, q_lora=qn.shape[1], kv_lora=kvn.shape[1]),
        grid=(b, lp // tm),
        in_specs=[pl.BlockSpec((1, tm, d), lambda bi, i: (bi, i, 0))]
                 + [_const_spec(c.shape) for c in (nw, wu, qn, kvn, wuq, wk, wvt)]
                 + [tab_spec] * 4,
        out_specs=[head_spec, head_spec,
                   pl.BlockSpec((1, MLA_HEADS, tm // KV_BLOCK, VT_ROWS, KV_BLOCK),
                                lambda bi, i: (bi, 0, i, 0, 0)),
                   pl.BlockSpec((1, tm, hg_cols), lambda bi, i: (bi, i, 0)),
                   pl.BlockSpec((1, tm, gate_cols), lambda bi, i: (bi, i, 0))],
        out_shape=[head_shape, head_shape,
                   jax.ShapeDtypeStruct((b, MLA_HEADS, lp // KV_BLOCK, VT_ROWS, KV_BLOCK), BF16),
                   jax.ShapeDtypeStruct((b, lp, hg_cols), F32),
                   jax.ShapeDtypeStruct((b, lp, gate_cols), BF16)],
        compiler_params=_params(("parallel", "parallel")),
        name="mix_in",
    )(h3d, nw, wu, qn, kvn, wuq, wk, wvt, *tabs)


def _attn_body(q_ref, k_ref, vt_ref, o_ref, acc_scr, m_scr, s_scr, qt_scr, *, n_groups):
    n_sub = SEQ_TILE // Q_BLOCK
    n_diag = SEQ_TILE // K_TILE
    vt_per_tile = K_TILE // KV_BLOCK
    key_i = lax.broadcasted_iota(jnp.int32, (K_TILE, Q_BLOCK), 0)
    qry_i = lax.broadcasted_iota(jnp.int32, (K_TILE, Q_BLOCK), 1)

    def needed(j, t):
        return j * K_TILE < (t + 1) * Q_BLOCK

    def scores(kb, slot, diag_j):
        for hh in range(2):
            k = k_ref[0, hh, pl.ds(pl.multiple_of(kb * K_TILE, K_TILE), K_TILE), :]
            for t in range(n_sub):
                if diag_j is not None and not needed(diag_j, t):
                    if diag_j == n_diag - 1:
                        s_scr[slot, hh, t] = jnp.full((K_TILE, Q_BLOCK), NEG_BIG, F32)
                    continue
                st = jnp.dot(k, qt_scr[hh, :, t * Q_BLOCK:(t + 1) * Q_BLOCK],
                             preferred_element_type=F32)
                if diag_j is not None and (diag_j + 1) * K_TILE > t * Q_BLOCK + 1:
                    st = jnp.where(key_i + diag_j * K_TILE <= qry_i + t * Q_BLOCK, st, NEG_BIG)
                s_scr[slot, hh, t] = st

    def update(kb, slot, diag_j):
        for hh in range(2):
            vt = jnp.concatenate([vt_ref[0, hh, kb * vt_per_tile + i] for i in range(vt_per_tile)], axis=1)
            for t in range(n_sub):
                if diag_j is not None and not needed(diag_j, t):
                    continue
                m_old = m_scr[hh, t]
                m_new = jnp.maximum(m_old, jnp.max(s_scr[slot, hh, t], axis=0, keepdims=True))
                p = jnp.exp2(s_scr[slot, hh, t] - m_new).astype(BF16)
                alpha = jnp.exp2(m_old - m_new)
                acc_scr[hh, t] = alpha * acc_scr[hh, t] + jnp.dot(vt, p, preferred_element_type=F32)
                m_scr[hh, t] = m_new

    def group(gi, carry):
        r0 = pl.multiple_of(gi * SEQ_TILE, SEQ_TILE)
        n_full = gi * n_diag
        acc_scr[...] = jnp.zeros(acc_scr.shape, F32)
        m_scr[...] = jnp.full(m_scr.shape, NEG_BIG, F32)
        for hh in range(2):
            for t in range(n_sub):
                qb = q_ref[0, hh, pl.ds(r0 + t * Q_BLOCK, Q_BLOCK), :]
                qt_scr[hh, :, t * Q_BLOCK:(t + 1) * Q_BLOCK] = qb.astype(F32).T.astype(BF16)

        scores(n_full, 0, 0)
        for j in range(1, n_diag):
            scores(n_full + j, j, j)
            update(n_full + j - 1, j - 1, j - 1)

        def kv_trip(i, c):
            kb = i * n_diag
            scores(kb, 0, None)
            update(jnp.where(i == 0, n_full + n_diag - 1, kb - 1), n_diag - 1, None)
            for j in range(1, n_diag):
                scores(kb + j, j, None)
                update(kb + j - 1, j - 1, None)
            return c

        lax.fori_loop(0, gi, kv_trip, 0)
        update(jnp.where(gi == 0, n_diag - 1, n_full - 1), n_diag - 1, None)
        for t in range(n_sub):
            outs = []
            for hh in range(2):
                a = acc_scr[hh, t]
                outs.append((a[:V_HEAD, :] / a[V_HEAD:V_HEAD + 1, :]).T)
            o_ref[0, pl.ds(r0 + t * Q_BLOCK, Q_BLOCK), :] = jnp.concatenate(outs, axis=-1).astype(BF16)
        return carry

    lax.fori_loop(0, n_groups, group, 0)


def _attn_call(q, k, vt):
    b, nh, lp, _ = q.shape
    qk_spec = pl.BlockSpec((1, 2, lp, LANES), lambda bi, hp: (bi, hp, 0, 0))
    vt_spec = pl.BlockSpec((1, 2) + vt.shape[2:], lambda bi, hp: (bi, hp, 0, 0, 0))
    n_sub = SEQ_TILE // Q_BLOCK
    return pl.pallas_call(
        functools.partial(_attn_body, n_groups=lp // SEQ_TILE),
        grid=(b, nh // 2),
        in_specs=[qk_spec, qk_spec, vt_spec],
        out_specs=pl.BlockSpec((1, lp, 2 * V_HEAD), lambda bi, hp: (bi, 0, hp)),
        out_shape=jax.ShapeDtypeStruct((b, lp, nh * V_HEAD), BF16),
        scratch_shapes=[pltpu.VMEM((2, n_sub, VT_ROWS, Q_BLOCK), F32),
                        pltpu.VMEM((2, n_sub, 1, Q_BLOCK), F32),
                        pltpu.VMEM((SEQ_TILE // K_TILE, 2, n_sub, K_TILE, Q_BLOCK), F32),
                        pltpu.VMEM((2, LANES, SEQ_TILE), BF16)],
        compiler_params=_params(("parallel", "parallel")),
        name="attn",
    )(q, k, vt)


def _split3_bf16(x):
    hi = x.astype(BF16)
    r1 = x - hi.astype(F32)
    mid = r1.astype(BF16)
    lo = (r1 - mid.astype(F32)).astype(BF16)
    return hi, mid, lo


def _hgrn_levels():
    w, out = HG_CHUNK // 2, []
    while w >= 1:
        out.append(w)
        w //= 2
    return out


def _hgrn_sum_matrix():
    c = HG_CHUNK
    t = np.arange(c)[:, None]
    s = np.arange(c)[None, :]
    blocks = [s <= t, s > t]
    for w in _hgrn_levels():
        r = (t // (2 * w)) * (2 * w) + w - 1
        is_q = (t // w) % 2 == 1
        blocks.append(np.where(is_q, (s > r) & (s <= t), (s > t) & (s <= r)))
    m = np.concatenate(blocks, axis=0).astype(np.float32)
    return jnp.asarray(np.concatenate([m, m, m], axis=1), dtype=BF16)


def _hgrn_body(hh_ref, lbraw_ref, nw_ref, p3_ref, o_ref, st_scr, *, layer):
    c = HG_CHUNK
    hw = HG_HEADS * HG_DIM
    n_chunks = hh_ref.shape[1] // c
    levels = _hgrn_levels()

    @pl.when(pl.program_id(1) == 0)
    def _():
        st_scr[...] = jnp.zeros(st_scr.shape, F32)

    raw = lbraw_ref[...]
    depth = raw.shape[0]
    rows = [raw[j:j + 1, :] for j in range(depth)]
    mx = functools.reduce(jnp.maximum, rows)
    ex = [jnp.exp(r - mx) for r in rows]
    inv_den = 1.0 / functools.reduce(jnp.add, ex)
    lb = jnp.zeros_like(mx)
    for j in range(1, layer + 1):
        lb = lb + ex[j] * inv_den
    one_m_lb = 1.0 - lb
    nw = nw_ref[...]
    scale = HG_DIM ** -0.5

    ri = lax.broadcasted_iota(jnp.int32, (c, c), 0)
    ci = lax.broadcasted_iota(jnp.int32, (c, c), 1)
    rowc = lax.broadcasted_iota(jnp.int32, (c, 1), 0)
    diag = ri == ci
    is_q = [(rowc // w) % 2 == 1 for w in levels]
    same_parent = [(ri // (2 * w)) == (ci // (2 * w)) for w in levels]

    def prepare(r0):
        xq = hh_ref[0, pl.ds(r0, c), 0:hw]
        z = hh_ref[0, pl.ds(r0, c), hw:2 * hw]
        vin = hh_ref[0, pl.ds(r0, c), 2 * hw:3 * hw]

        e = jnp.exp(-jnp.abs(z))
        r = 1.0 / (1.0 + e)
        er = e * r
        pos = z >= 0
        f = lb + one_m_lb * jnp.where(pos, r, er)
        logf2 = jnp.log(jnp.maximum(f, F_MIN)) * LOG2E
        kk = one_m_lb * jnp.where(pos, er, r)
        qq = xq * _sigmoid(xq) * scale
        expo = jnp.dot(p3_ref[...], jnp.concatenate(_split3_bf16(logf2), axis=0),
                       preferred_element_type=F32)
        return qq, kk, vin.astype(BF16), jnp.exp2(expo)

    def products(qq, kk, vb, dec):
        heads = []
        for hd in range(HG_HEADS):
            sl = slice(hd * HG_DIM, (hd + 1) * HG_DIM)
            qh, kh, vh = qq[:, sl], kk[:, sl], vb[:, sl]
            blk = lambda i: dec[i * c:(i + 1) * c, sl]
            st = st_scr[hd]
            d_cum = blk(0)
            parts = [lax.dot_general(qh.astype(BF16), kh.astype(BF16), _NT, preferred_element_type=F32)]
            for i, w in enumerate(levels):
                d_w = blk(2 + i)
                qt = jnp.where(is_q[i], qh * d_w, 0.0).astype(BF16)
                kt = jnp.where(is_q[i], 0.0, kh * d_w).astype(BF16)
                parts.append(lax.dot_general(qt, kt, _NT, preferred_element_type=F32))
            qe = (qh * d_cum).astype(BF16)
            o_st = lax.dot_general(qe, st.astype(BF16), _NT, preferred_element_type=F32)
            kd = (kh * blk(1)).astype(BF16)
            st_scr[hd] = st * d_cum[c - 1:c, :] + lax.dot_general(vh, kd, _TN, preferred_element_type=F32)
            heads.append((sl, vh, parts, o_st))
        return heads

    def finish(r0, heads):
        for sl, vh, parts, o_st in heads:
            a = jnp.where(diag, parts[0], 0.0)
            for i, w in enumerate(levels):
                a = a + (parts[1 + i] if 2 * w == c else jnp.where(same_parent[i], parts[1 + i], 0.0))
            o_acc = o_st + jnp.dot(a.astype(BF16), vh, preferred_element_type=F32)
            gh = hh_ref[0, pl.ds(r0, c), 3 * hw + sl.start:3 * hw + sl.stop]
            y = _rms(o_acc, nw) * (gh * _sigmoid(gh))
            o_ref[0, pl.ds(r0, c), sl] = y.astype(BF16)

    def chunks(it, carry):
        starts = [pl.multiple_of((it * HG_UNROLL + j) * c, c) for j in range(HG_UNROLL)]
        prepared = [prepare(r0) for r0 in starts]
        produced = [products(*pr) for pr in prepared]
        for r0, heads in zip(starts, produced):
            finish(r0, heads)
        return carry

    lax.fori_loop(0, n_chunks // HG_UNROLL, chunks, 0)


def _hgrn_call(hh, lb_raw, nw, layer):
    b, lp, cols = hh.shape
    hw = HG_HEADS * HG_DIM
    t = SEQ_TILE
    p3 = _hgrn_sum_matrix()
    return pl.pallas_call(
        functools.partial(_hgrn_body, layer=layer),
        grid=(b, lp // t),
        in_specs=[pl.BlockSpec((1, t, cols), lambda bi, i: (bi, i, 0)),
                  _const_spec(lb_raw.shape), _const_spec(nw.shape), _const_spec(p3.shape)],
        out_specs=pl.BlockSpec((1, t, hw), lambda bi, i: (bi, i, 0)),
        out_shape=jax.ShapeDtypeStruct((b, lp, hw), BF16),
        scratch_shapes=[pltpu.VMEM((HG_HEADS, HG_DIM, HG_DIM), F32)],
        compiler_params=_params(("parallel", "arbitrary")),
        name="hgrn2",
    )(hh, lb_raw, nw, p3)


def _rope_tables(lp):
    half = QK_ROPE // 2
    inv = ROPE_THETA ** (-jnp.arange(half, dtype=F32) / half)
    ang = jnp.arange(lp).astype(F32)[:, None] * inv[None, :]
    cos, sin = jnp.cos(ang), jnp.sin(ang)
    zpad = jnp.zeros((lp, LANES - QK_NOPE - QK_ROPE), F32)
    qscale = math.log2(math.e) * (QK_NOPE + QK_ROPE) ** -0.5
    cos_q = jnp.concatenate([jnp.ones((lp, QK_NOPE), F32), cos, cos, zpad], axis=1) * qscale
    sin_s = jnp.concatenate([jnp.zeros((lp, QK_NOPE), F32), -sin, sin, zpad], axis=1)
    cos_k = jnp.concatenate([jnp.zeros((lp, QK_NOPE), F32), cos, cos, zpad], axis=1)
    return cos_q, sin_s * qscale, cos_k, sin_s


def _swap_halves(w):
    half = w.shape[-1] // 2
    return jnp.concatenate([w[..., half:], w[..., :half]], axis=-1)


def _prep_mixer_weights(w_in, w_uq, w_ukv, q_lora, kv_lora):
    depth, d, _ = w_in.shape
    hg = HG_HEADS * HG_DIM
    o_pe = q_lora + kv_lora
    o_hg = o_pe + QK_ROPE
    w_pe = w_in[:, :, o_pe:o_hg]
    zl = jnp.zeros((depth, d, QK_NOPE), F32)
    zr = jnp.zeros((depth, d, LANES - QK_NOPE - QK_ROPE), F32)
    wu = jnp.concatenate([w_in[:, :, :o_pe], zl, w_pe, zr, zl, _swap_halves(w_pe), zr,
                          w_in[:, :, o_hg:]], axis=2).astype(BF16)

    dqk = QK_NOPE + QK_ROPE
    uq = w_uq.reshape(depth, q_lora, MLA_HEADS, dqk)
    zq = jnp.zeros((depth, q_lora, MLA_HEADS, LANES - dqk), F32)
    uq_plain = jnp.concatenate([uq, zq], axis=3)
    uq_swap = jnp.concatenate([jnp.zeros_like(uq[..., :QK_NOPE]), _swap_halves(uq[..., QK_NOPE:]), zq], axis=3)
    wuq = jnp.concatenate([uq_plain.reshape(depth, q_lora, -1), uq_swap.reshape(depth, q_lora, -1)],
                          axis=2).astype(BF16)

    ukv = w_ukv.reshape(depth, kv_lora, MLA_HEADS, QK_NOPE + V_HEAD)
    zk = jnp.zeros((depth, kv_lora, MLA_HEADS, LANES - QK_NOPE), F32)
    zv = jnp.zeros((depth, kv_lora, MLA_HEADS, VT_ROWS - V_HEAD), F32)
    wk = jnp.concatenate([ukv[..., :QK_NOPE], zk], axis=3).reshape(depth, kv_lora, -1).astype(BF16)
    wv = jnp.concatenate([ukv[..., QK_NOPE:], zv], axis=3).reshape(depth, kv_lora, -1)
    wvt = jnp.swapaxes(wv, 1, 2).astype(BF16)
    return wu, wuq, wk, wvt, 4 * hg


def kernel(x, meta_tokens, ffn1_norm, ffn1_w_gu, ffn1_w_down, mix_norm, w_in, q_norm, kv_norm,
           w_uq, w_ukv, hg_lb_raw, hg_norm, w_proj_attn, w_proj_rec, w_out,
           ffn2_norm, ffn2_w_gu, ffn2_w_down, final_norm):
    b, seq, d = x.shape
    depth = w_in.shape[0]
    q_lora, kv_lora = q_norm.shape[1], kv_norm.shape[1]
    l = N_META + seq
    lp = -(-l // SEQ_TILE) * SEQ_TILE
    assert (b * lp) % ROW_TILE == 0 and lp % MIX_TILE == 0

    meta = jnp.broadcast_to(meta_tokens.astype(x.dtype)[None], (b, N_META, d))
    h = jnp.concatenate([meta, x, jnp.zeros((b, lp - l, d), x.dtype)], axis=1)

    wgu1, wgu2 = ffn1_w_gu.astype(BF16), ffn2_w_gu.astype(BF16)
    wd1, wd2 = (0.5 * ffn1_w_down).astype(BF16), (0.5 * ffn2_w_down).astype(BF16)
    wpa, wpr, wo = w_proj_attn.astype(BF16), w_proj_rec.astype(BF16), w_out.astype(BF16)
    wu, wuq, wk, wvt, hg_cols = _prep_mixer_weights(w_in, w_uq, w_ukv, q_lora, kv_lora)
    tabs = _rope_tables(lp)
    row2 = lambda a: a[:, None, :]
    n1, nm, n2, qn, kvn, hgn = map(row2, (ffn1_norm, mix_norm, ffn2_norm, q_norm, kv_norm, hg_norm))
    fnw = final_norm[None, :]
    lb_raw = hg_lb_raw.astype(F32)

    h2 = h.reshape(b * lp, d)
    for layer in range(depth):
        h2 = _ffn_call(h2, n1[layer], wgu1[layer], wd1[layer])
        q, k, vt, hh, gates = _mix_in_call(h2.reshape(b, lp, d), nm[layer], wu[layer], qn[layer],
                                           kvn[layer], wuq[layer], wk[layer], wvt[layer], tabs, hg_cols)
        o_attn = _attn_call(q, k, vt)
        o_rec = _hgrn_call(hh, lb_raw, hgn[layer], layer)
        h2 = _merge_ffn_call(h2, o_attn.reshape(b * lp, -1), o_rec.reshape(b * lp, -1),
                             gates.reshape(b * lp, -1), wpa[layer], wpr[layer], wo[layer],
                             n2[layer], wgu2[layer], wd2[layer], fnw, final=(layer == depth - 1))
    return h2.reshape(b, lp, d)[:, N_META:l]
```

```python
import functools
import math
from typing import NamedTuple

import numpy as np
import jax
import jax.numpy as jnp
from jax import lax
from jax.experimental import pallas as pl
from jax.experimental.pallas import tpu as pltpu

F32 = jnp.float32
BF16 = jnp.bfloat16

N_META = 16
MLA_HEADS = 8
QK_NOPE = 64
QK_ROPE = 32
V_HEAD = 64
ROPE_THETA = 10000.0
HG_HEADS = 4
HG_DIM = 128
EPS = 1e-6
F_MIN = 1e-20
NEG_BIG = -1e30
LOG2E = math.log2(math.e)
Q_SCALE = LOG2E * (QK_NOPE + QK_ROPE) ** -0.5

LANES = 128
SUBLANES = 8
MXU_TILE = 256
VMEM_LIMIT_BYTES = 56 * 1024 * 1024

SEQ_TILE = 768
Q_BLOCK = 256
K_TILE = 256
KV_BLOCK = 128
ATTN_UNROLL = 2
VT_ROWS = 80
MIX_TILE = 384
ROW_TILE = 512
HG_CHUNK = 64
HG_UNROLL = 2
FFN_CHUNKS = 2

_NT = (((1,), (1,)), ((), ()))
_TN = (((0,), (0,)), ((), ()))


def _sigmoid(x):
    return 1.0 / (1.0 + jnp.exp(-x))


def _rms(x, w):
    return x * lax.rsqrt(jnp.mean(x * x, axis=-1, keepdims=True) + EPS) * w


class _Layer(NamedTuple):
    stacked: jax.Array
    layer: int

    @property
    def shape(self):
        return self.stacked.shape[1:]


def _operand(p):
    return p.stacked if isinstance(p, _Layer) else p


def _const_spec(p):
    nd = len(p.shape)
    if isinstance(p, _Layer):
        return pl.BlockSpec((None,) + tuple(p.shape), lambda *_: (p.layer,) + (0,) * nd,
                            pipeline_mode=pl.Buffered(1))
    return pl.BlockSpec(tuple(p.shape), lambda *_: (0,) * nd, pipeline_mode=pl.Buffered(1))


def _params(sem):
    return pltpu.CompilerParams(dimension_semantics=sem, vmem_limit_bytes=VMEM_LIMIT_BYTES)


def _ffn_chunks(d_ff):
    n_tiles = d_ff // MXU_TILE
    bounds = [round(i * n_tiles / FFN_CHUNKS) * MXU_TILE for i in range(FFN_CHUNKS + 1)]
    return list(zip(bounds[:-1], bounds[1:]))


def _ffn_value(x, nw, wgu_ref, wd_ref):
    d_ff = wd_ref.shape[0]
    xn = _rms(x, nw).astype(BF16)
    acc = x
    for lo, hi in _ffn_chunks(d_ff):
        g = jnp.dot(xn, wgu_ref[:, lo:hi], preferred_element_type=F32)
        u = jnp.dot(xn, wgu_ref[:, d_ff + lo:d_ff + hi], preferred_element_type=F32)
        a = (g * _sigmoid(g) * u).astype(BF16)
        acc = acc + jnp.dot(a, wd_ref[lo:hi, :], preferred_element_type=F32)
    return acc


def _ffn_body(x_ref, nw_ref, wgu_ref, wd_ref, o_ref):
    o_ref[...] = _ffn_value(x_ref[...], nw_ref[...], wgu_ref, wd_ref)


def _ffn_call(h2d, nw, wgu, wd):
    rows, d = h2d.shape
    return pl.pallas_call(
        _ffn_body,
        grid=(rows // ROW_TILE,),
        in_specs=[
            pl.BlockSpec((ROW_TILE, d), lambda i: (i, 0)),
            _const_spec(nw),
            _const_spec(wgu),
            _const_spec(wd),
        ],
        out_specs=pl.BlockSpec((ROW_TILE, d), lambda i: (i, 0)),
        out_shape=jax.ShapeDtypeStruct((rows, d), F32),
        compiler_params=_params(("parallel",)),
        name="ffn",
    )(h2d, _operand(nw), _operand(wgu), _operand(wd))


def _merge_ffn_body(h_ref, oa_ref, ob_ref, g_ref, wpa_ref, wpr_ref, wo_ref,
                    nw_ref, wgu_ref, wd_ref, fnw_ref, o_ref, *, final):
    d = h_ref.shape[1]
    ya = jnp.dot(oa_ref[...], wpa_ref[...], preferred_element_type=F32)
    yb = jnp.dot(ob_ref[...], wpr_ref[...], preferred_element_type=F32)
    merged = g_ref[:, :d].astype(F32) * ya + g_ref[:, d:].astype(F32) * yb
    h = h_ref[...] + jnp.dot(merged.astype(BF16), wo_ref[...], preferred_element_type=F32)
    h = _ffn_value(h, nw_ref[...], wgu_ref, wd_ref)
    if final:
        h = _rms(h, fnw_ref[...])
    o_ref[...] = h


def _merge_ffn_call(h2d, oa, ob, gates, wpa, wpr, wo, nw, wgu, wd, fnw, final):
    rows, d = h2d.shape
    row_spec = lambda a: pl.BlockSpec((ROW_TILE, a.shape[1]), lambda i: (i, 0))
    consts = (wpa, wpr, wo, nw, wgu, wd, fnw)
    return pl.pallas_call(
        functools.partial(_merge_ffn_body, final=final),
        grid=(rows // ROW_TILE,),
        in_specs=[row_spec(h2d), row_spec(oa), row_spec(ob), row_spec(gates)]
                 + [_const_spec(c) for c in consts],
        out_specs=pl.BlockSpec((ROW_TILE, d), lambda i: (i, 0)),
        out_shape=jax.ShapeDtypeStruct((rows, d), F32),
        compiler_params=_params(("parallel",)),
        name="merge_ffn",
    )(h2d, oa, ob, gates, *map(_operand, consts))


def _mix_in_body(h_ref, nw_ref, wu_ref, qn_ref, kvn_ref, wuqt_ref, wk_ref, wvt_ref,
                 cqt_ref, sqt_ref, ck_ref, sk_ref,
                 qt_ref, k_ref, vt_ref, hh_ref, g_ref, *, q_lora, kv_lora):
    o_kv = q_lora
    o_pe = o_kv + kv_lora
    o_hg = o_pe + 2 * LANES
    o_gt = o_hg + hh_ref.shape[2]
    u = _rms(h_ref[0], nw_ref[...]).astype(BF16)

    hh_ref[0] = jnp.dot(u, wu_ref[:, o_hg:o_gt], preferred_element_type=F32)
    g_ref[0] = _sigmoid(jnp.dot(u, wu_ref[:, o_gt:], preferred_element_type=F32)).astype(BF16)

    cq = jnp.dot(u, wu_ref[:, :o_kv], preferred_element_type=F32)
    cqn = _rms(cq, qn_ref[...]).astype(BF16)
    qft = lax.dot_general(wuqt_ref[...], cqn, _NT, preferred_element_type=F32)
    cq_t, sq_t = cqt_ref[...], sqt_ref[...]
    half = QK_ROPE // 2
    for hd in range(MLA_HEADS):
        blk = qft[hd * LANES:(hd + 1) * LANES]
        x1, x2 = blk[QK_NOPE:QK_NOPE + half], blk[QK_NOPE + half:QK_NOPE + QK_ROPE]
        qh = jnp.concatenate([blk[:QK_NOPE] * Q_SCALE, x1 * cq_t - x2 * sq_t, x2 * cq_t + x1 * sq_t,
                              blk[QK_NOPE + QK_ROPE:]], axis=0).astype(BF16)
        for jj in range(qh.shape[1] // KV_BLOCK):
            qt_ref[0, hd, jj] = qh[:, jj * KV_BLOCK:(jj + 1) * KV_BLOCK]

    ckv = jnp.dot(u, wu_ref[:, o_kv:o_pe], preferred_element_type=F32)
    ckvn = _rms(ckv, kvn_ref[...]).astype(BF16)
    kpe = jnp.dot(u, wu_ref[:, o_pe:o_hg], preferred_element_type=F32)
    krot = kpe[:, :LANES] * ck_ref[...] + kpe[:, LANES:] * sk_ref[...]
    kf = jnp.dot(ckvn, wk_ref[...], preferred_element_type=F32)
    for hd in range(MLA_HEADS):
        sl = slice(hd * LANES, (hd + 1) * LANES)
        k_ref[0, hd] = (kf[:, sl] + krot).astype(BF16)
    vt = lax.dot_general(wvt_ref[...], ckvn, _NT, preferred_element_type=F32)
    ones_row = (lax.broadcasted_iota(jnp.int32, (vt.shape[0], 1), 0) % VT_ROWS == V_HEAD).astype(F32)
    vt = (vt + ones_row).astype(BF16)
    for hd in range(MLA_HEADS):
        for jj in range(vt.shape[1] // KV_BLOCK):
            vt_ref[0, hd, jj] = vt[hd * VT_ROWS:(hd + 1) * VT_ROWS, jj * KV_BLOCK:(jj + 1) * KV_BLOCK]


def _mix_in_call(h3d, nw, wu, qn, kvn, wuqt, wk, wvt, tabs, hg_cols):
    b, lp, d = h3d.shape
    tm = MIX_TILE
    head_shape = jax.ShapeDtypeStruct((b, MLA_HEADS, lp, LANES), BF16)
    head_spec = pl.BlockSpec((1, MLA_HEADS, tm, LANES), lambda bi, i: (bi, 0, i, 0))
    tab_spec = pl.BlockSpec((tm, LANES), lambda bi, i: (i, 0))
    tabt_spec = pl.BlockSpec((QK_ROPE // 2, tm), lambda bi, i: (0, i))
    qt_spec = pl.BlockSpec((1, MLA_HEADS, tm // KV_BLOCK, LANES, KV_BLOCK), lambda bi, i: (bi, 0, i, 0, 0))
    gate_cols = wu.shape[1] - (qn.shape[1] + kvn.shape[1] + 2 * LANES + hg_cols)
    return pl.pallas_call(
        functools.partial(_mix_in_body, q_lora=qn.shape[1], kv_lora=kvn.shape[1]),
        grid=(b, lp // tm),
        in_specs=[pl.BlockSpec((1, tm, d), lambda bi, i: (bi, i, 0))]
                 + [_const_spec(c) for c in (nw, wu, qn, kvn, wuqt, wk, wvt)]
                 + [tabt_spec] * 2 + [tab_spec] * 2,
        out_specs=[qt_spec, head_spec,
                   pl.BlockSpec((1, MLA_HEADS, tm // KV_BLOCK, VT_ROWS, KV_BLOCK),
                                lambda bi, i: (bi, 0, i, 0, 0)),
                   pl.BlockSpec((1, tm, hg_cols), lambda bi, i: (bi, i, 0)),
                   pl.BlockSpec((1, tm, gate_cols), lambda bi, i: (bi, i, 0))],
        out_shape=[jax.ShapeDtypeStruct((b, MLA_HEADS, lp // KV_BLOCK, LANES, KV_BLOCK), BF16), head_shape,
                   jax.ShapeDtypeStruct((b, MLA_HEADS, lp // KV_BLOCK, VT_ROWS, KV_BLOCK), BF16),
                   jax.ShapeDtypeStruct((b, lp, hg_cols), F32),
                   jax.ShapeDtypeStruct((b, lp, gate_cols), BF16)],
        compiler_params=_params(("parallel", "parallel")),
        name="mix_in",
    )(h3d, *map(_operand, (nw, wu, qn, kvn, wuqt, wk, wvt)), *tabs)


def _attn_body(qt_ref, k_ref, vt_ref, o_ref, acc_scr, m_scr, s_scr, mx_scr, fin_scr, *, n_groups):
    n_sub = SEQ_TILE // Q_BLOCK
    n_diag = SEQ_TILE // K_TILE
    vt_per_tile = K_TILE // KV_BLOCK
    q_per_blk = Q_BLOCK // KV_BLOCK
    key_i = lax.broadcasted_iota(jnp.int32, (K_TILE, Q_BLOCK), 0)
    qry_i = lax.broadcasted_iota(jnp.int32, (K_TILE, Q_BLOCK), 1)

    def needed(j, t):
        return j * K_TILE < (t + 1) * Q_BLOCK

    def scores(qb0, kb, slot, diag_j):
        for hh in range(2):
            k = k_ref[0, hh, pl.ds(pl.multiple_of(kb * K_TILE, K_TILE), K_TILE), :]
            for t in range(n_sub):
                if diag_j is not None and not needed(diag_j, t):
                    continue
                qt = jnp.concatenate([qt_ref[0, hh, qb0 + t * q_per_blk + i] for i in range(q_per_blk)], axis=1)
                st = jnp.dot(k, qt, preferred_element_type=F32)
                if diag_j is not None and (diag_j + 1) * K_TILE > t * Q_BLOCK + 1:
                    st = jnp.where(key_i + diag_j * K_TILE <= qry_i + t * Q_BLOCK, st, NEG_BIG)
                s_scr[slot, hh, t] = st
                mx_scr[slot, hh, t] = jnp.max(st, axis=0, keepdims=True)

    def update(kb, slot, diag_j):
        for hh in range(2):
            vt = jnp.concatenate([vt_ref[0, hh, kb * vt_per_tile + i] for i in range(vt_per_tile)], axis=1)
            for t in range(n_sub):
                if diag_j is not None and not needed(diag_j, t):
                    continue
                m_old = m_scr[hh, t]
                m_new = jnp.maximum(m_old, mx_scr[slot, hh, t])
                p = jnp.exp2(s_scr[slot, hh, t] - m_new).astype(BF16)
                alpha = jnp.exp2(m_old - m_new)
                acc_scr[hh, t] = alpha * acc_scr[hh, t] + jnp.dot(vt, p, preferred_element_type=F32)
                m_scr[hh, t] = m_new

    def finalize(r0):
        for t in range(n_sub):
            outs = []
            for hh in range(2):
                a = fin_scr[hh, t]
                outs.append((a[:V_HEAD, :] / a[V_HEAD:V_HEAD + 1, :]).T)
            o_ref[0, pl.ds(r0 + t * Q_BLOCK, Q_BLOCK), :] = jnp.concatenate(outs, axis=-1).astype(BF16)

    def group(gi, carry):
        r0 = pl.multiple_of(gi * SEQ_TILE, SEQ_TILE)
        qb0 = gi * (SEQ_TILE // KV_BLOCK)
        n_full = gi * n_diag
        acc_scr[...] = jnp.zeros(acc_scr.shape, F32)
        m_scr[...] = jnp.full(m_scr.shape, NEG_BIG, F32)

        diag_order = list(range(n_diag - 1, -1, -1))
        scores(qb0, n_full + diag_order[0], 0, diag_order[0])
        finalize(pl.multiple_of(jnp.maximum(gi - 1, 0) * SEQ_TILE, SEQ_TILE))
        for pos in range(1, n_diag):
            scores(qb0, n_full + diag_order[pos], pos, diag_order[pos])
            update(n_full + diag_order[pos - 1], pos - 1, diag_order[pos - 1])

        def kv_trip(i):
            kb = i * n_diag
            scores(qb0, kb, 0, None)
            update(jnp.where(i == 0, n_full, kb - 1), n_diag - 1, None)
            for j in range(1, n_diag):
                scores(qb0, kb + j, j, None)
                update(kb + j - 1, j - 1, None)

        def kv_trips(ii, c):
            for u in range(ATTN_UNROLL):
                kv_trip(ii * ATTN_UNROLL + u)
            return c

        lax.fori_loop(0, gi // ATTN_UNROLL, kv_trips, 0)
        for u in range(ATTN_UNROLL - 1):
            @pl.when(gi % ATTN_UNROLL > u)
            def _():
                kv_trip((gi // ATTN_UNROLL) * ATTN_UNROLL + u)
        update(jnp.where(gi == 0, 0, n_full - 1), n_diag - 1, None)
        fin_scr[...] = acc_scr[...]
        return carry

    fin_scr[...] = jnp.ones(fin_scr.shape, F32)
    lax.fori_loop(0, n_groups, group, 0)
    finalize((n_groups - 1) * SEQ_TILE)


def _attn_call(qt, k, vt):
    b, nh, lp, _ = k.shape
    k_spec = pl.BlockSpec((1, 2, lp, LANES), lambda bi, hp: (bi, hp, 0, 0))
    blk_spec = lambda a: pl.BlockSpec((1, 2) + a.shape[2:], lambda bi, hp: (bi, hp, 0, 0, 0))
    n_sub = SEQ_TILE // Q_BLOCK
    return pl.pallas_call(
        functools.partial(_attn_body, n_groups=lp // SEQ_TILE),
        grid=(b, nh // 2),
        in_specs=[blk_spec(qt), k_spec, blk_spec(vt)],
        out_specs=pl.BlockSpec((1, lp, 2 * V_HEAD), lambda bi, hp: (bi, 0, hp)),
        out_shape=jax.ShapeDtypeStruct((b, lp, nh * V_HEAD), BF16),
        scratch_shapes=[pltpu.VMEM((2, n_sub, VT_ROWS, Q_BLOCK), F32),
                        pltpu.VMEM((2, n_sub, 1, Q_BLOCK), F32),
                        pltpu.VMEM((SEQ_TILE // K_TILE, 2, n_sub, K_TILE, Q_BLOCK), F32),
                        pltpu.VMEM((SEQ_TILE // K_TILE, 2, n_sub, 1, Q_BLOCK), F32),
                        pltpu.VMEM((2, n_sub, VT_ROWS, Q_BLOCK), F32)],
        compiler_params=_params(("parallel", "parallel")),
        name="attn",
    )(qt, k, vt)


def _split3_bf16(x):
    hi = x.astype(BF16)
    r1 = x - hi.astype(F32)
    mid = r1.astype(BF16)
    lo = (r1 - mid.astype(F32)).astype(BF16)
    return hi, mid, lo


def _hgrn_levels():
    w, out = HG_CHUNK // 2, []
    while w >= 1:
        out.append(w)
        w //= 2
    return out


def _hgrn_sum_matrix():
    c = HG_CHUNK
    t = np.arange(c)[:, None]
    s = np.arange(c)[None, :]
    blocks = [s <= t, s > t]
    for w in _hgrn_levels():
        r = (t // (2 * w)) * (2 * w) + w - 1
        is_q = (t // w) % 2 == 1
        blocks.append(np.where(is_q, (s > r) & (s <= t), (s > t) & (s <= r)))
    m = np.concatenate(blocks, axis=0).astype(np.float32)
    return jnp.asarray(np.concatenate([m, m, m], axis=1), dtype=BF16)


def _hgrn_body(hh_ref, lbraw_ref, nw_ref, p3_ref, o_ref, st_scr, *, layer):
    c = HG_CHUNK
    hw = HG_HEADS * HG_DIM
    n_chunks = hh_ref.shape[1] // c
    levels = _hgrn_levels()

    @pl.when(pl.program_id(1) == 0)
    def _():
        st_scr[...] = jnp.zeros(st_scr.shape, F32)

    raw = lbraw_ref[...]
    depth = raw.shape[0]
    rows = [raw[j:j + 1, :] for j in range(depth)]
    mx = functools.reduce(jnp.maximum, rows)
    ex = [jnp.exp(r - mx) for r in rows]
    inv_den = 1.0 / functools.reduce(jnp.add, ex)
    lb = jnp.zeros_like(mx)
    for j in range(1, layer + 1):
        lb = lb + ex[j] * inv_den
    one_m_lb = 1.0 - lb
    nw = nw_ref[...]
    scale = HG_DIM ** -0.5

    ri = lax.broadcasted_iota(jnp.int32, (c, c), 0)
    ci = lax.broadcasted_iota(jnp.int32, (c, c), 1)
    rowc = lax.broadcasted_iota(jnp.int32, (c, 1), 0)
    diag = ri == ci
    is_q = [(rowc // w) % 2 == 1 for w in levels]
    same_parent = [(ri // (2 * w)) == (ci // (2 * w)) for w in levels]

    def prepare(r0):
        xq = hh_ref[0, pl.ds(r0, c), 0:hw]
        z = hh_ref[0, pl.ds(r0, c), hw:2 * hw]
        vin = hh_ref[0, pl.ds(r0, c), 2 * hw:3 * hw]

        e = jnp.exp(-jnp.abs(z))
        r = 1.0 / (1.0 + e)
        er = e * r
        pos = z >= 0
        f = lb + one_m_lb * jnp.where(pos, r, er)
        logf2 = jnp.log(jnp.maximum(f, F_MIN)) * LOG2E
        kk = one_m_lb * jnp.where(pos, er, r)
        qq = xq * _sigmoid(xq) * scale
        expo = jnp.dot(p3_ref[...], jnp.concatenate(_split3_bf16(logf2), axis=0),
                       preferred_element_type=F32)
        return qq, kk, vin.astype(BF16), jnp.exp2(expo)

    def products(qq, kk, vb, dec):
        heads = []
        for hd in range(HG_HEADS):
            sl = slice(hd * HG_DIM, (hd + 1) * HG_DIM)
            qh, kh, vh = qq[:, sl], kk[:, sl], vb[:, sl]
            blk = lambda i: dec[i * c:(i + 1) * c, sl]
            st = st_scr[hd]
            d_cum = blk(0)
            parts = [lax.dot_general(qh.astype(BF16), kh.astype(BF16), _NT, preferred_element_type=F32)]
            for i, w in enumerate(levels):
                d_w = blk(2 + i)
                qt = jnp.where(is_q[i], qh * d_w, 0.0).astype(BF16)
                kt = jnp.where(is_q[i], 0.0, kh * d_w).astype(BF16)
                parts.append(lax.dot_general(qt, kt, _NT, preferred_element_type=F32))
            qe = (qh * d_cum).astype(BF16)
            o_st = lax.dot_general(qe, st.astype(BF16), _NT, preferred_element_type=F32)
            kd = (kh * blk(1)).astype(BF16)
            st_scr[hd] = st * d_cum[c - 1:c, :] + lax.dot_general(vh, kd, _TN, preferred_element_type=F32)
            heads.append((sl, vh, parts, o_st))
        return heads

    def finish(r0, heads):
        for sl, vh, parts, o_st in heads:
            a = jnp.where(diag, parts[0], 0.0)
            for i, w in enumerate(levels):
                a = a + (parts[1 + i] if 2 * w == c else jnp.where(same_parent[i], parts[1 + i], 0.0))
            o_acc = o_st + jnp.dot(a.astype(BF16), vh, preferred_element_type=F32)
            gh = hh_ref[0, pl.ds(r0, c), 3 * hw + sl.start:3 * hw + sl.stop]
            y = _rms(o_acc, nw) * (gh * _sigmoid(gh))
            o_ref[0, pl.ds(r0, c), sl] = y.astype(BF16)

    def chunks(it, carry):
        starts = [pl.multiple_of((it * HG_UNROLL + j) * c, c) for j in range(HG_UNROLL)]
        prepared = [prepare(r0) for r0 in starts]
        produced = [products(*pr) for pr in prepared]
        for r0, heads in zip(starts, produced):
            finish(r0, heads)
        return carry

    lax.fori_loop(0, n_chunks // HG_UNROLL, chunks, 0)


def _hgrn_call(hh, lb_raw, nw, layer):
    b, lp, cols = hh.shape
    hw = HG_HEADS * HG_DIM
    t = SEQ_TILE
    p3 = _hgrn_sum_matrix()
    return pl.pallas_call(
        functools.partial(_hgrn_body, layer=layer),
        grid=(b, lp // t),
        in_specs=[pl.BlockSpec((1, t, cols), lambda bi, i: (bi, i, 0)),
                  _const_spec(lb_raw), _const_spec(nw), _const_spec(p3)],
        out_specs=pl.BlockSpec((1, t, hw), lambda bi, i: (bi, i, 0)),
        out_shape=jax.ShapeDtypeStruct((b, lp, hw), BF16),
        scratch_shapes=[pltpu.VMEM((HG_HEADS, HG_DIM, HG_DIM), F32)],
        compiler_params=_params(("parallel", "arbitrary")),
        name="hgrn2",
    )(hh, lb_raw, _operand(nw), p3)


def _rope_tables(lp):
    half = QK_ROPE // 2
    inv = ROPE_THETA ** (-jnp.arange(half, dtype=F32) / half)
    ang = jnp.arange(lp).astype(F32)[:, None] * inv[None, :]
    cos, sin = jnp.cos(ang), jnp.sin(ang)
    zpad = jnp.zeros((lp, LANES - QK_NOPE - QK_ROPE), F32)
    sin_s = jnp.concatenate([jnp.zeros((lp, QK_NOPE), F32), -sin, sin, zpad], axis=1)
    cos_k = jnp.concatenate([jnp.zeros((lp, QK_NOPE), F32), cos, cos, zpad], axis=1)
    return (cos * Q_SCALE).T, (sin * Q_SCALE).T, cos_k, sin_s


def _swap_halves(w):
    half = w.shape[-1] // 2
    return jnp.concatenate([w[..., half:], w[..., :half]], axis=-1)


def _prep_mixer_weights(w_in, w_uq, w_ukv, q_lora, kv_lora):
    depth, d, _ = w_in.shape
    hg = HG_HEADS * HG_DIM
    o_pe = q_lora + kv_lora
    o_hg = o_pe + QK_ROPE
    w_pe = w_in[:, :, o_pe:o_hg]
    zl = jnp.zeros((depth, d, QK_NOPE), F32)
    zr = jnp.zeros((depth, d, LANES - QK_NOPE - QK_ROPE), F32)
    wu = jnp.concatenate([w_in[:, :, :o_pe], zl, w_pe, zr, zl, _swap_halves(w_pe), zr,
                          w_in[:, :, o_hg:]], axis=2).astype(BF16)

    dqk = QK_NOPE + QK_ROPE
    uq = w_uq.reshape(depth, q_lora, MLA_HEADS, dqk)
    zq = jnp.zeros((depth, q_lora, MLA_HEADS, LANES - dqk), F32)
    wuqt = jnp.swapaxes(jnp.concatenate([uq, zq], axis=3).reshape(depth, q_lora, -1), 1, 2).astype(BF16)

    ukv = w_ukv.reshape(depth, kv_lora, MLA_HEADS, QK_NOPE + V_HEAD)
    zk = jnp.zeros((depth, kv_lora, MLA_HEADS, LANES - QK_NOPE), F32)
    zv = jnp.zeros((depth, kv_lora, MLA_HEADS, VT_ROWS - V_HEAD), F32)
    wk = jnp.concatenate([ukv[..., :QK_NOPE], zk], axis=3).reshape(depth, kv_lora, -1).astype(BF16)
    wv = jnp.concatenate([ukv[..., QK_NOPE:], zv], axis=3).reshape(depth, kv_lora, -1)
    wvt = jnp.swapaxes(wv, 1, 2).astype(BF16)
    return wu, wuqt, wk, wvt, 4 * hg


def kernel(x, meta_tokens, ffn1_norm, ffn1_w_gu, ffn1_w_down, mix_norm, w_in, q_norm, kv_norm,
           w_uq, w_ukv, hg_lb_raw, hg_norm, w_proj_attn, w_proj_rec, w_out,
           ffn2_norm, ffn2_w_gu, ffn2_w_down, final_norm):
    b, seq, d = x.shape
    depth = w_in.shape[0]
    q_lora, kv_lora = q_norm.shape[1], kv_norm.shape[1]
    l = N_META + seq
    lp = -(-l // SEQ_TILE) * SEQ_TILE
    assert (b * lp) % ROW_TILE == 0 and lp % MIX_TILE == 0

    meta = jnp.broadcast_to(meta_tokens.astype(x.dtype)[None], (b, N_META, d))
    h = jnp.concatenate([meta, x, jnp.zeros((b, lp - l, d), x.dtype)], axis=1)

    wgu1, wgu2 = ffn1_w_gu.astype(BF16), ffn2_w_gu.astype(BF16)
    wd1, wd2 = (0.5 * ffn1_w_down).astype(BF16), (0.5 * ffn2_w_down).astype(BF16)
    wpa, wpr, wo = w_proj_attn.astype(BF16), w_proj_rec.astype(BF16), w_out.astype(BF16)
    wu, wuqt, wk, wvt, hg_cols = _prep_mixer_weights(w_in, w_uq, w_ukv, q_lora, kv_lora)
    tabs = _rope_tables(lp)
    row2 = lambda a: a[:, None, :]
    n1, nm, n2, qn, kvn, hgn = map(row2, (ffn1_norm, mix_norm, ffn2_norm, q_norm, kv_norm, hg_norm))
    fnw = final_norm[None, :]
    lb_raw = hg_lb_raw.astype(F32)

    h2 = h.reshape(b * lp, d)
    for layer in range(depth):
        at = lambda a: _Layer(a, layer)
        h2 = _ffn_call(h2, at(n1), at(wgu1), at(wd1))
        qt, k, vt, hh, gates = _mix_in_call(h2.reshape(b, lp, d), at(nm), at(wu), at(qn), at(kvn),
                                            at(wuqt), at(wk), at(wvt), tabs, hg_cols)
        o_attn = _attn_call(qt, k, vt)
        o_rec = _hgrn_call(hh, lb_raw, at(hgn), layer)
        h2 = _merge_ffn_call(h2, o_attn.reshape(b * lp, -1), o_rec.reshape(b * lp, -1),
                             gates.reshape(b * lp, -1), at(wpa), at(wpr), at(wo),
                             at(n2), at(wgu2), at(wd2), fnw, final=(layer == depth - 1))
    return h2.reshape(b, lp, d)[:, N_META:l]
```

```python
import functools
import math
from typing import NamedTuple

import numpy as np
import jax
import jax.numpy as jnp
from jax import lax
from jax.experimental import pallas as pl
from jax.experimental.pallas import tpu as pltpu

F32 = jnp.float32
BF16 = jnp.bfloat16

N_META = 16
MLA_HEADS = 8
QK_NOPE = 64
QK_ROPE = 32
V_HEAD = 64
ROPE_THETA = 10000.0
HG_HEADS = 4
HG_DIM = 128
EPS = 1e-6
F_MIN = 1e-20
NEG_BIG = -1e30
LOG2E = math.log2(math.e)
Q_SCALE = LOG2E * (QK_NOPE + QK_ROPE) ** -0.5

LANES = 128
SUBLANES = 8
MXU_TILE = 256
VMEM_LIMIT_BYTES = 56 * 1024 * 1024

SEQ_TILE = 768
Q_BLOCK = 256
K_TILE = 256
KV_BLOCK = 128
ATTN_UNROLL = 2
VT_ROWS = 80
MIX_TILE = 384
ROW_TILE = 512
HG_CHUNK = 64
HG_UNROLL = 2
FFN_CHUNKS = 2
FFN_SPLIT = 2

_NT = (((1,), (1,)), ((), ()))
_TN = (((0,), (0,)), ((), ()))


def _sigmoid(x):
    return 1.0 / (1.0 + jnp.exp(-x))


def _rms(x, w):
    return x * lax.rsqrt(jnp.mean(x * x, axis=-1, keepdims=True) + EPS) * w


class _Layer(NamedTuple):
    stacked: jax.Array
    layer: int

    @property
    def shape(self):
        return self.stacked.shape[1:]


def _operand(p):
    return p.stacked if isinstance(p, _Layer) else p


def _const_spec(p):
    nd = len(p.shape)
    if isinstance(p, _Layer):
        return pl.BlockSpec((None,) + tuple(p.shape), lambda *_: (p.layer,) + (0,) * nd,
                            pipeline_mode=pl.Buffered(1))
    return pl.BlockSpec(tuple(p.shape), lambda *_: (0,) * nd, pipeline_mode=pl.Buffered(1))


def _params(sem):
    return pltpu.CompilerParams(dimension_semantics=sem, vmem_limit_bytes=VMEM_LIMIT_BYTES)


def _ffn_chunks(d_ff):
    n_tiles = d_ff // MXU_TILE
    bounds = [round(i * n_tiles / FFN_CHUNKS) * MXU_TILE for i in range(FFN_CHUNKS + 1)]
    return list(zip(bounds[:-1], bounds[1:]))


def _ffn_value(x, nw, wgu_ref, wd_ref):
    d_ff = wd_ref.shape[0]
    rows = x.shape[0] // FFN_SPLIT
    accs = [x[i * rows:(i + 1) * rows] for i in range(FFN_SPLIT)]
    xns = [_rms(a, nw).astype(BF16) for a in accs]
    for lo, hi in _ffn_chunks(d_ff):
        gs = [jnp.dot(xn, wgu_ref[:, lo:hi], preferred_element_type=F32) for xn in xns]
        us = [jnp.dot(xn, wgu_ref[:, d_ff + lo:d_ff + hi], preferred_element_type=F32) for xn in xns]
        acts = [(g * _sigmoid(g) * u).astype(BF16) for g, u in zip(gs, us)]
        accs = [acc + jnp.dot(a, wd_ref[lo:hi, :], preferred_element_type=F32) for acc, a in zip(accs, acts)]
    return jnp.concatenate(accs, axis=0)


def _ffn_body(x_ref, nw_ref, wgu_ref, wd_ref, o_ref):
    o_ref[...] = _ffn_value(x_ref[...], nw_ref[...], wgu_ref, wd_ref)


def _ffn_call(h2d, nw, wgu, wd):
    rows, d = h2d.shape
    return pl.pallas_call(
        _ffn_body,
        grid=(rows // ROW_TILE,),
        in_specs=[
            pl.BlockSpec((ROW_TILE, d), lambda i: (i, 0)),
            _const_spec(nw),
            _const_spec(wgu),
            _const_spec(wd),
        ],
        out_specs=pl.BlockSpec((ROW_TILE, d), lambda i: (i, 0)),
        out_shape=jax.ShapeDtypeStruct((rows, d), F32),
        compiler_params=_params(("parallel",)),
        name="ffn",
    )(h2d, _operand(nw), _operand(wgu), _operand(wd))


def _merge_ffn_body(h_ref, oa_ref, ob_ref, g_ref, wpa_ref, wpr_ref, wo_ref,
                    nw_ref, wgu_ref, wd_ref, fnw_ref, o_ref, *, final):
    d = h_ref.shape[1]
    ya = jnp.dot(oa_ref[...], wpa_ref[...], preferred_element_type=F32)
    yb = jnp.dot(ob_ref[...], wpr_ref[...], preferred_element_type=F32)
    merged = g_ref[:, :d].astype(F32) * ya + g_ref[:, d:].astype(F32) * yb
    h = h_ref[...] + jnp.dot(merged.astype(BF16), wo_ref[...], preferred_element_type=F32)
    h = _ffn_value(h, nw_ref[...], wgu_ref, wd_ref)
    if final:
        h = _rms(h, fnw_ref[...])
    o_ref[...] = h


def _merge_ffn_call(h2d, oa, ob, gates, wpa, wpr, wo, nw, wgu, wd, fnw, final):
    rows, d = h2d.shape
    row_spec = lambda a: pl.BlockSpec((ROW_TILE, a.shape[1]), lambda i: (i, 0))
    consts = (wpa, wpr, wo, nw, wgu, wd, fnw)
    return pl.pallas_call(
        functools.partial(_merge_ffn_body, final=final),
        grid=(rows // ROW_TILE,),
        in_specs=[row_spec(h2d), row_spec(oa), row_spec(ob), row_spec(gates)]
                 + [_const_spec(c) for c in consts],
        out_specs=pl.BlockSpec((ROW_TILE, d), lambda i: (i, 0)),
        out_shape=jax.ShapeDtypeStruct((rows, d), F32),
        compiler_params=_params(("parallel",)),
        name="merge_ffn",
    )(h2d, oa, ob, gates, *map(_operand, consts))


def _merge_ffn_out_call(h3d, oa, ob, gates, wpa, wpr, wo, nw, wgu, wd, fnw, first, count):
    b, _, d = h3d.shape
    row_spec = lambda a: pl.BlockSpec((pl.Element(1), pl.Element(ROW_TILE), pl.Element(a.shape[2])),
                                      lambda bi, i: (bi, pl.multiple_of(first + i * ROW_TILE, math.gcd(first, ROW_TILE)), 0))
    consts = (wpa, wpr, wo, nw, wgu, wd, fnw)

    def body(h_ref, oa_ref, ob_ref, g_ref, *rest):
        _merge_ffn_body(h_ref.at[0], oa_ref.at[0], ob_ref.at[0], g_ref.at[0], *rest, final=True)

    return pl.pallas_call(
        body,
        grid=(b, count // ROW_TILE),
        in_specs=[row_spec(h3d), row_spec(oa), row_spec(ob), row_spec(gates)]
                 + [_const_spec(c) for c in consts],
        out_specs=pl.BlockSpec((None, ROW_TILE, d), lambda bi, i: (bi, i, 0)),
        out_shape=jax.ShapeDtypeStruct((b, count, d), F32),
        compiler_params=_params(("parallel", "parallel")),
        name="merge_ffn_out",
    )(h3d, oa, ob, gates, *map(_operand, consts))


def _mix_in_body(h_ref, nw_ref, wu_ref, wb_ref, qn_ref, kvn_ref, wuqt_ref, wk_ref, wvt_ref,
                 cqt_ref, sqt_ref, ck_ref, sk_ref,
                 qt_ref, k_ref, vt_ref, hh_ref, g_ref, *, q_lora, kv_lora):
    o_kv = q_lora
    o_pe = o_kv + kv_lora
    o_gt = hh_ref.shape[2]
    u = _rms(h_ref[0], nw_ref[...]).astype(BF16)

    cq = jnp.dot(u, wu_ref[:, :o_kv], preferred_element_type=F32)
    ckv = jnp.dot(u, wu_ref[:, o_kv:o_pe], preferred_element_type=F32)
    kpe = jnp.dot(u, wu_ref[:, o_pe:], preferred_element_type=F32)
    cqn = _rms(cq, qn_ref[...]).astype(BF16)
    ckvn = _rms(ckv, kvn_ref[...]).astype(BF16)

    hh_ref[0] = jnp.dot(u, wb_ref[:, :o_gt], preferred_element_type=F32)
    g_ref[0] = _sigmoid(jnp.dot(u, wb_ref[:, o_gt:], preferred_element_type=F32)).astype(BF16)

    qft = lax.dot_general(wuqt_ref[...], cqn, _NT, preferred_element_type=F32)
    cq_t, sq_t = cqt_ref[...], sqt_ref[...]
    half = QK_ROPE // 2
    for hd in range(MLA_HEADS):
        blk = qft[hd * LANES:(hd + 1) * LANES]
        x1, x2 = blk[QK_NOPE:QK_NOPE + half], blk[QK_NOPE + half:QK_NOPE + QK_ROPE]
        qh = jnp.concatenate([blk[:QK_NOPE] * Q_SCALE, x1 * cq_t - x2 * sq_t, x2 * cq_t + x1 * sq_t,
                              blk[QK_NOPE + QK_ROPE:]], axis=0).astype(BF16)
        for jj in range(qh.shape[1] // KV_BLOCK):
            qt_ref[0, hd, jj] = qh[:, jj * KV_BLOCK:(jj + 1) * KV_BLOCK]

    krot = kpe[:, :LANES] * ck_ref[...] + kpe[:, LANES:] * sk_ref[...]
    kf = jnp.dot(ckvn, wk_ref[...], preferred_element_type=F32)
    for hd in range(MLA_HEADS):
        sl = slice(hd * LANES, (hd + 1) * LANES)
        k_ref[0, hd] = (kf[:, sl] + krot).astype(BF16)
    vt = lax.dot_general(wvt_ref[...], ckvn, _NT, preferred_element_type=F32)
    ones_row = (lax.broadcasted_iota(jnp.int32, (vt.shape[0], 1), 0) % VT_ROWS == V_HEAD).astype(F32)
    vt = (vt + ones_row).astype(BF16)
    for hd in range(MLA_HEADS):
        for jj in range(vt.shape[1] // KV_BLOCK):
            vt_ref[0, hd, jj] = vt[hd * VT_ROWS:(hd + 1) * VT_ROWS, jj * KV_BLOCK:(jj + 1) * KV_BLOCK]


def _mix_in_call(h3d, nw, wu, wb, qn, kvn, wuqt, wk, wvt, tabs, hg_cols):
    b, lp, d = h3d.shape
    tm = MIX_TILE
    head_shape = jax.ShapeDtypeStruct((b, MLA_HEADS, lp, LANES), BF16)
    head_spec = pl.BlockSpec((1, MLA_HEADS, tm, LANES), lambda bi, i: (bi, 0, i, 0))
    tab_spec = pl.BlockSpec((tm, LANES), lambda bi, i: (i, 0))
    tabt_spec = pl.BlockSpec((QK_ROPE // 2, tm), lambda bi, i: (0, i))
    qt_spec = pl.BlockSpec((1, MLA_HEADS, tm // KV_BLOCK, LANES, KV_BLOCK), lambda bi, i: (bi, 0, i, 0, 0))
    gate_cols = wb.shape[1] - hg_cols
    return pl.pallas_call(
        functools.partial(_mix_in_body, q_lora=qn.shape[1], kv_lora=kvn.shape[1]),
        grid=(b, lp // tm),
        in_specs=[pl.BlockSpec((1, tm, d), lambda bi, i: (bi, i, 0))]
                 + [_const_spec(c) for c in (nw, wu, wb, qn, kvn, wuqt, wk, wvt)]
                 + [tabt_spec] * 2 + [tab_spec] * 2,
        out_specs=[qt_spec, head_spec,
                   pl.BlockSpec((1, MLA_HEADS, tm // KV_BLOCK, VT_ROWS, KV_BLOCK),
                                lambda bi, i: (bi, 0, i, 0, 0)),
                   pl.BlockSpec((1, tm, hg_cols), lambda bi, i: (bi, i, 0)),
                   pl.BlockSpec((1, tm, gate_cols), lambda bi, i: (bi, i, 0))],
        out_shape=[jax.ShapeDtypeStruct((b, MLA_HEADS, lp // KV_BLOCK, LANES, KV_BLOCK), BF16), head_shape,
                   jax.ShapeDtypeStruct((b, MLA_HEADS, lp // KV_BLOCK, VT_ROWS, KV_BLOCK), BF16),
                   jax.ShapeDtypeStruct((b, lp, hg_cols), F32),
                   jax.ShapeDtypeStruct((b, lp, gate_cols), BF16)],
        compiler_params=_params(("parallel", "parallel")),
        name="mix_in",
    )(h3d, *map(_operand, (nw, wu, wb, qn, kvn, wuqt, wk, wvt)), *tabs)


def _attn_body(qt_ref, k_ref, vt_ref, o_ref, acc_scr, m_scr, s_scr, mx_scr, fin_scr, *, n_groups):
    n_sub = SEQ_TILE // Q_BLOCK
    n_diag = SEQ_TILE // K_TILE
    vt_per_tile = K_TILE // KV_BLOCK
    q_per_blk = Q_BLOCK // KV_BLOCK
    key_i = lax.broadcasted_iota(jnp.int32, (K_TILE, Q_BLOCK), 0)
    qry_i = lax.broadcasted_iota(jnp.int32, (K_TILE, Q_BLOCK), 1)

    def needed(j, t):
        return j * K_TILE < (t + 1) * Q_BLOCK

    def scores(qb0, kb, slot, diag_j):
        for hh in range(2):
            k = k_ref[0, hh, pl.ds(pl.multiple_of(kb * K_TILE, K_TILE), K_TILE), :]
            for t in range(n_sub):
                if diag_j is not None and not needed(diag_j, t):
                    continue
                qt = jnp.concatenate([qt_ref[0, hh, qb0 + t * q_per_blk + i] for i in range(q_per_blk)], axis=1)
                st = jnp.dot(k, qt, preferred_element_type=F32)
                if diag_j is not None and (diag_j + 1) * K_TILE > t * Q_BLOCK + 1:
                    st = jnp.where(key_i + diag_j * K_TILE <= qry_i + t * Q_BLOCK, st, NEG_BIG)
                s_scr[slot, hh, t] = st
                mx_scr[slot, hh, t] = jnp.max(st, axis=0, keepdims=True)

    def update(kb, slot, diag_j):
        for hh in range(2):
            vt = jnp.concatenate([vt_ref[0, hh, kb * vt_per_tile + i] for i in range(vt_per_tile)], axis=1)
            for t in range(n_sub):
                if diag_j is not None and not needed(diag_j, t):
                    continue
                m_old = m_scr[hh, t]
                m_new = jnp.maximum(m_old, mx_scr[slot, hh, t])
                p = jnp.exp2(s_scr[slot, hh, t] - m_new).astype(BF16)
                alpha = jnp.exp2(m_old - m_new)
                acc_scr[hh, t] = alpha * acc_scr[hh, t] + jnp.dot(vt, p, preferred_element_type=F32)
                m_scr[hh, t] = m_new

    def finalize(r0):
        for t in range(n_sub):
            outs = []
            for hh in range(2):
                a = fin_scr[hh, t]
                outs.append((a[:V_HEAD, :] / a[V_HEAD:V_HEAD + 1, :]).T)
            o_ref[0, pl.ds(r0 + t * Q_BLOCK, Q_BLOCK), :] = jnp.concatenate(outs, axis=-1).astype(BF16)

    def group(gi, carry):
        r0 = pl.multiple_of(gi * SEQ_TILE, SEQ_TILE)
        qb0 = gi * (SEQ_TILE // KV_BLOCK)
        n_full = gi * n_diag
        acc_scr[...] = jnp.zeros(acc_scr.shape, F32)
        m_scr[...] = jnp.full(m_scr.shape, NEG_BIG, F32)

        diag_order = list(range(n_diag - 1, -1, -1))
        scores(qb0, n_full + diag_order[0], 0, diag_order[0])
        finalize(pl.multiple_of(jnp.maximum(gi - 1, 0) * SEQ_TILE, SEQ_TILE))
        for pos in range(1, n_diag):
            scores(qb0, n_full + diag_order[pos], pos, diag_order[pos])
            update(n_full + diag_order[pos - 1], pos - 1, diag_order[pos - 1])

        def kv_trip(i):
            kb = i * n_diag
            scores(qb0, kb, 0, None)
            update(jnp.where(i == 0, n_full, kb - 1), n_diag - 1, None)
            for j in range(1, n_diag):
                scores(qb0, kb + j, j, None)
                update(kb + j - 1, j - 1, None)

        def kv_trips(ii, c):
            for u in range(ATTN_UNROLL):
                kv_trip(ii * ATTN_UNROLL + u)
            return c

        lax.fori_loop(0, gi // ATTN_UNROLL, kv_trips, 0)
        for u in range(ATTN_UNROLL - 1):
            @pl.when(gi % ATTN_UNROLL > u)
            def _():
                kv_trip((gi // ATTN_UNROLL) * ATTN_UNROLL + u)
        update(jnp.where(gi == 0, 0, n_full - 1), n_diag - 1, None)
        fin_scr[...] = acc_scr[...]
        return carry

    fin_scr[...] = jnp.ones(fin_scr.shape, F32)
    lax.fori_loop(0, n_groups, group, 0)
    finalize((n_groups - 1) * SEQ_TILE)


def _attn_call(qt, k, vt):
    b, nh, lp, _ = k.shape
    k_spec = pl.BlockSpec((1, 2, lp, LANES), lambda bi, hp: (bi, hp, 0, 0))
    blk_spec = lambda a: pl.BlockSpec((1, 2) + a.shape[2:], lambda bi, hp: (bi, hp, 0, 0, 0))
    n_sub = SEQ_TILE // Q_BLOCK
    return pl.pallas_call(
        functools.partial(_attn_body, n_groups=lp // SEQ_TILE),
        grid=(b, nh // 2),
        in_specs=[blk_spec(qt), k_spec, blk_spec(vt)],
        out_specs=pl.BlockSpec((1, lp, 2 * V_HEAD), lambda bi, hp: (bi, 0, hp)),
        out_shape=jax.ShapeDtypeStruct((b, lp, nh * V_HEAD), BF16),
        scratch_shapes=[pltpu.VMEM((2, n_sub, VT_ROWS, Q_BLOCK), F32),
                        pltpu.VMEM((2, n_sub, 1, Q_BLOCK), F32),
                        pltpu.VMEM((SEQ_TILE // K_TILE, 2, n_sub, K_TILE, Q_BLOCK), F32),
                        pltpu.VMEM((SEQ_TILE // K_TILE, 2, n_sub, 1, Q_BLOCK), F32),
                        pltpu.VMEM((2, n_sub, VT_ROWS, Q_BLOCK), F32)],
        compiler_params=_params(("parallel", "parallel")),
        name="attn",
    )(qt, k, vt)


def _split3_bf16(x):
    hi = x.astype(BF16)
    r1 = x - hi.astype(F32)
    mid = r1.astype(BF16)
    lo = (r1 - mid.astype(F32)).astype(BF16)
    return hi, mid, lo


def _hgrn_levels():
    w, out = HG_CHUNK // 2, []
    while w >= 1:
        out.append(w)
        w //= 2
    return out


def _hgrn_sum_matrix():
    c = HG_CHUNK
    t = np.arange(c)[:, None]
    s = np.arange(c)[None, :]
    blocks = [s <= t, s > t]
    for w in _hgrn_levels():
        r = (t // (2 * w)) * (2 * w) + w - 1
        is_q = (t // w) % 2 == 1
        blocks.append(np.where(is_q, (s > r) & (s <= t), (s > t) & (s <= r)))
    m = np.concatenate(blocks, axis=0).astype(np.float32)
    return jnp.asarray(np.concatenate([m, m, m], axis=1), dtype=BF16)


def _hgrn_body(hh_ref, lbraw_ref, nw_ref, p3_ref, o_ref, st_scr, *, layer):
    c = HG_CHUNK
    hw = HG_HEADS * HG_DIM
    n_chunks = hh_ref.shape[1] // c
    levels = _hgrn_levels()

    @pl.when(pl.program_id(1) == 0)
    def _():
        st_scr[...] = jnp.zeros(st_scr.shape, F32)

    raw = lbraw_ref[...]
    depth = raw.shape[0]
    rows = [raw[j:j + 1, :] for j in range(depth)]
    mx = functools.reduce(jnp.maximum, rows)
    ex = [jnp.exp(r - mx) for r in rows]
    inv_den = 1.0 / functools.reduce(jnp.add, ex)
    lb = jnp.zeros_like(mx)
    for j in range(1, layer + 1):
        lb = lb + ex[j] * inv_den
    one_m_lb = 1.0 - lb
    nw = nw_ref[...]
    scale = HG_DIM ** -0.5

    ri = lax.broadcasted_iota(jnp.int32, (c, c), 0)
    ci = lax.broadcasted_iota(jnp.int32, (c, c), 1)
    rowc = lax.broadcasted_iota(jnp.int32, (c, 1), 0)
    diag = ri == ci
    is_q = [(rowc // w) % 2 == 1 for w in levels]
    same_parent = [(ri // (2 * w)) == (ci // (2 * w)) for w in levels]

    def prepare(r0):
        xq = hh_ref[0, pl.ds(r0, c), 0:hw]
        z = hh_ref[0, pl.ds(r0, c), hw:2 * hw]
        vin = hh_ref[0, pl.ds(r0, c), 2 * hw:3 * hw]

        e = jnp.exp(-jnp.abs(z))
        r = 1.0 / (1.0 + e)
        er = e * r
        pos = z >= 0
        f = lb + one_m_lb * jnp.where(pos, r, er)
        logf2 = jnp.log(jnp.maximum(f, F_MIN)) * LOG2E
        kk = one_m_lb * jnp.where(pos, er, r)
        qq = xq * _sigmoid(xq) * scale
        expo = jnp.dot(p3_ref[...], jnp.concatenate(_split3_bf16(logf2), axis=0),
                       preferred_element_type=F32)
        return qq, kk, vin.astype(BF16), jnp.exp2(expo)

    def products(qq, kk, vb, dec):
        heads = []
        for hd in range(HG_HEADS):
            sl = slice(hd * HG_DIM, (hd + 1) * HG_DIM)
            qh, kh, vh = qq[:, sl], kk[:, sl], vb[:, sl]
            blk = lambda i: dec[i * c:(i + 1) * c, sl]
            st = st_scr[hd]
            d_cum = blk(0)
            parts = [lax.dot_general(qh.astype(BF16), kh.astype(BF16), _NT, preferred_element_type=F32)]
            for i, w in enumerate(levels):
                d_w = blk(2 + i)
                qt = jnp.where(is_q[i], qh * d_w, 0.0).astype(BF16)
                kt = jnp.where(is_q[i], 0.0, kh * d_w).astype(BF16)
                parts.append(lax.dot_general(qt, kt, _NT, preferred_element_type=F32))
            qe = (qh * d_cum).astype(BF16)
            o_st = lax.dot_general(qe, st.astype(BF16), _NT, preferred_element_type=F32)
            kd = (kh * blk(1)).astype(BF16)
            st_scr[hd] = st * d_cum[c - 1:c, :] + lax.dot_general(vh, kd, _TN, preferred_element_type=F32)
            heads.append((sl, vh, parts, o_st))
        return heads

    def finish(r0, heads):
        for sl, vh, parts, o_st in heads:
            a = jnp.where(diag, parts[0], 0.0)
            for i, w in enumerate(levels):
                a = a + (parts[1 + i] if 2 * w == c else jnp.where(same_parent[i], parts[1 + i], 0.0))
            o_acc = o_st + jnp.dot(a.astype(BF16), vh, preferred_element_type=F32)
            gh = hh_ref[0, pl.ds(r0, c), 3 * hw + sl.start:3 * hw + sl.stop]
            y = _rms(o_acc, nw) * (gh * _sigmoid(gh))
            o_ref[0, pl.ds(r0, c), sl] = y.astype(BF16)

    def chunks(it, carry):
        starts = [pl.multiple_of((it * HG_UNROLL + j) * c, c) for j in range(HG_UNROLL)]
        prepared = [prepare(r0) for r0 in starts]
        produced = [products(*pr) for pr in prepared]
        for r0, heads in zip(starts, produced):
            finish(r0, heads)
        return carry

    lax.fori_loop(0, n_chunks // HG_UNROLL, chunks, 0)


def _hgrn_call(hh, lb_raw, nw, layer):
    b, lp, cols = hh.shape
    hw = HG_HEADS * HG_DIM
    t = SEQ_TILE
    p3 = _hgrn_sum_matrix()
    return pl.pallas_call(
        functools.partial(_hgrn_body, layer=layer),
        grid=(b, lp // t),
        in_specs=[pl.BlockSpec((1, t, cols), lambda bi, i: (bi, i, 0)),
                  _const_spec(lb_raw), _const_spec(nw), _const_spec(p3)],
        out_specs=pl.BlockSpec((1, t, hw), lambda bi, i: (bi, i, 0)),
        out_shape=jax.ShapeDtypeStruct((b, lp, hw), BF16),
        scratch_shapes=[pltpu.VMEM((HG_HEADS, HG_DIM, HG_DIM), F32)],
        compiler_params=_params(("parallel", "arbitrary")),
        name="hgrn2",
    )(hh, lb_raw, _operand(nw), p3)


def _rope_tables(lp):
    half = QK_ROPE // 2
    inv = ROPE_THETA ** (-jnp.arange(half, dtype=F32) / half)
    ang = jnp.arange(lp).astype(F32)[:, None] * inv[None, :]
    cos, sin = jnp.cos(ang), jnp.sin(ang)
    zpad = jnp.zeros((lp, LANES - QK_NOPE - QK_ROPE), F32)
    sin_s = jnp.concatenate([jnp.zeros((lp, QK_NOPE), F32), -sin, sin, zpad], axis=1)
    cos_k = jnp.concatenate([jnp.zeros((lp, QK_NOPE), F32), cos, cos, zpad], axis=1)
    return (cos * Q_SCALE).T, (sin * Q_SCALE).T, cos_k, sin_s


def _swap_halves(w):
    half = w.shape[-1] // 2
    return jnp.concatenate([w[..., half:], w[..., :half]], axis=-1)


def _prep_mixer_weights(w_in, w_uq, w_ukv, q_lora, kv_lora):
    depth, d, _ = w_in.shape
    hg = HG_HEADS * HG_DIM
    o_pe = q_lora + kv_lora
    o_hg = o_pe + QK_ROPE
    w_pe = w_in[:, :, o_pe:o_hg]
    zl = jnp.zeros((depth, d, QK_NOPE), F32)
    zr = jnp.zeros((depth, d, LANES - QK_NOPE - QK_ROPE), F32)
    wu = jnp.concatenate([w_in[:, :, :o_pe], zl, w_pe, zr, zl, _swap_halves(w_pe), zr], axis=2).astype(BF16)
    wb = w_in[:, :, o_hg:].astype(BF16)

    dqk = QK_NOPE + QK_ROPE
    uq = w_uq.reshape(depth, q_lora, MLA_HEADS, dqk)
    zq = jnp.zeros((depth, q_lora, MLA_HEADS, LANES - dqk), F32)
    wuqt = jnp.swapaxes(jnp.concatenate([uq, zq], axis=3).reshape(depth, q_lora, -1), 1, 2).astype(BF16)

    ukv = w_ukv.reshape(depth, kv_lora, MLA_HEADS, QK_NOPE + V_HEAD)
    zk = jnp.zeros((depth, kv_lora, MLA_HEADS, LANES - QK_NOPE), F32)
    zv = jnp.zeros((depth, kv_lora, MLA_HEADS, VT_ROWS - V_HEAD), F32)
    wk = jnp.concatenate([ukv[..., :QK_NOPE], zk], axis=3).reshape(depth, kv_lora, -1).astype(BF16)
    wv = jnp.concatenate([ukv[..., QK_NOPE:], zv], axis=3).reshape(depth, kv_lora, -1)
    wvt = jnp.swapaxes(wv, 1, 2).astype(BF16)
    return wu, wb, wuqt, wk, wvt, 4 * hg


def kernel(x, meta_tokens, ffn1_norm, ffn1_w_gu, ffn1_w_down, mix_norm, w_in, q_norm, kv_norm,
           w_uq, w_ukv, hg_lb_raw, hg_norm, w_proj_attn, w_proj_rec, w_out,
           ffn2_norm, ffn2_w_gu, ffn2_w_down, final_norm):
    b, seq, d = x.shape
    depth = w_in.shape[0]
    q_lora, kv_lora = q_norm.shape[1], kv_norm.shape[1]
    l = N_META + seq
    lp = -(-l // SEQ_TILE) * SEQ_TILE
    assert (b * lp) % ROW_TILE == 0 and lp % MIX_TILE == 0

    meta = jnp.broadcast_to(meta_tokens.astype(x.dtype)[None], (b, N_META, d))
    h = jnp.concatenate([meta, x, jnp.zeros((b, lp - l, d), x.dtype)], axis=1)

    wgu1, wgu2 = ffn1_w_gu.astype(BF16), ffn2_w_gu.astype(BF16)
    wd1, wd2 = (0.5 * ffn1_w_down).astype(BF16), (0.5 * ffn2_w_down).astype(BF16)
    wpa, wpr, wo = w_proj_attn.astype(BF16), w_proj_rec.astype(BF16), w_out.astype(BF16)
    wu, wb, wuqt, wk, wvt, hg_cols = _prep_mixer_weights(w_in, w_uq, w_ukv, q_lora, kv_lora)
    tabs = _rope_tables(lp)
    row2 = lambda a: a[:, None, :]
    n1, nm, n2, qn, kvn, hgn = map(row2, (ffn1_norm, mix_norm, ffn2_norm, q_norm, kv_norm, hg_norm))
    fnw = final_norm[None, :]
    lb_raw = hg_lb_raw.astype(F32)

    h2 = h.reshape(b * lp, d)
    for layer in range(depth):
        at = lambda a: _Layer(a, layer)
        h2 = _ffn_call(h2, at(n1), at(wgu1), at(wd1))
        qt, k, vt, hh, gates = _mix_in_call(h2.reshape(b, lp, d), at(nm), at(wu), at(wb), at(qn), at(kvn),
                                            at(wuqt), at(wk), at(wvt), tabs, hg_cols)
        o_attn = _attn_call(qt, k, vt)
        o_rec = _hgrn_call(hh, lb_raw, at(hgn), layer)
        tail = (at(wpa), at(wpr), at(wo), at(n2), at(wgu2), at(wd2), fnw)
        if layer == depth - 1 and seq % ROW_TILE == 0:
            return _merge_ffn_out_call(h2.reshape(b, lp, d), o_attn, o_rec, gates, *tail,
                                       first=N_META, count=seq)
        h2 = _merge_ffn_call(h2, o_attn.reshape(b * lp, -1), o_rec.reshape(b * lp, -1),
                             gates.reshape(b * lp, -1), *tail, final=(layer == depth - 1))
    return h2.reshape(b, lp, d)[:, N_META:l]
```

```python
import functools
import math
from typing import NamedTuple

import numpy as np
import jax
import jax.numpy as jnp
from jax import lax
from jax.experimental import pallas as pl
from jax.experimental.pallas import tpu as pltpu

F32 = jnp.float32
BF16 = jnp.bfloat16

N_META = 16
MLA_HEADS = 8
QK_NOPE = 64
QK_ROPE = 32
V_HEAD = 64
ROPE_THETA = 10000.0
HG_HEADS = 4
HG_DIM = 128
EPS = 1e-6
F_MIN = 1e-20
NEG_BIG = -1e30
LOG2E = math.log2(math.e)
Q_SCALE = LOG2E * (QK_NOPE + QK_ROPE) ** -0.5

LANES = 128
SUBLANES = 8
MXU_TILE = 256
VMEM_LIMIT_BYTES = 56 * 1024 * 1024

SEQ_TILE = 768
Q_BLOCK = 256
K_TILE = 256
KV_BLOCK = 128
ATTN_UNROLL = 2
VT_ROWS = 80
MIX_TILE = 384
ROW_TILE = 512
HG_CHUNK = 64
HG_UNROLL = 2
FFN_CHUNKS = 2
FFN_SPLIT = 2

_NT = (((1,), (1,)), ((), ()))
_TN = (((0,), (0,)), ((), ()))


def _sigmoid(x):
    return 1.0 / (1.0 + jnp.exp(-x))


def _rms(x, w):
    return x * lax.rsqrt(jnp.mean(x * x, axis=-1, keepdims=True) + EPS) * w


class _Layer(NamedTuple):
    stacked: jax.Array
    layer: int

    @property
    def shape(self):
        return self.stacked.shape[1:]


def _operand(p):
    return p.stacked if isinstance(p, _Layer) else p


def _const_spec(p):
    nd = len(p.shape)
    if isinstance(p, _Layer):
        return pl.BlockSpec((None,) + tuple(p.shape), lambda *_: (p.layer,) + (0,) * nd,
                            pipeline_mode=pl.Buffered(1))
    return pl.BlockSpec(tuple(p.shape), lambda *_: (0,) * nd, pipeline_mode=pl.Buffered(1))


def _params(sem):
    return pltpu.CompilerParams(dimension_semantics=sem, vmem_limit_bytes=VMEM_LIMIT_BYTES)


def _ffn_chunks(d_ff):
    n_tiles = d_ff // MXU_TILE
    bounds = [round(i * n_tiles / FFN_CHUNKS) * MXU_TILE for i in range(FFN_CHUNKS + 1)]
    return list(zip(bounds[:-1], bounds[1:]))


def _row_groups(n_rows):
    rows = n_rows // FFN_SPLIT
    return [slice(i * rows, (i + 1) * rows) for i in range(FFN_SPLIT)]


def _ffn_value(xs, nw, wgu_ref, wd_ref):
    d_ff = wd_ref.shape[0]
    accs = list(xs)
    xns = [_rms(a, nw).astype(BF16) for a in accs]
    for lo, hi in _ffn_chunks(d_ff):
        gs = [jnp.dot(xn, wgu_ref[:, lo:hi], preferred_element_type=F32) for xn in xns]
        us = [jnp.dot(xn, wgu_ref[:, d_ff + lo:d_ff + hi], preferred_element_type=F32) for xn in xns]
        acts = [(g * _sigmoid(g) * u).astype(BF16) for g, u in zip(gs, us)]
        accs = [acc + jnp.dot(a, wd_ref[lo:hi, :], preferred_element_type=F32) for acc, a in zip(accs, acts)]
    return jnp.concatenate(accs, axis=0)


def _ffn_body(x_ref, nw_ref, wgu_ref, wd_ref, o_ref):
    xs = [x_ref[g, :] for g in _row_groups(x_ref.shape[0])]
    o_ref[...] = _ffn_value(xs, nw_ref[...], wgu_ref, wd_ref)


def _ffn_call(h2d, nw, wgu, wd):
    rows, d = h2d.shape
    return pl.pallas_call(
        _ffn_body,
        grid=(rows // ROW_TILE,),
        in_specs=[
            pl.BlockSpec((ROW_TILE, d), lambda i: (i, 0)),
            _const_spec(nw),
            _const_spec(wgu),
            _const_spec(wd),
        ],
        out_specs=pl.BlockSpec((ROW_TILE, d), lambda i: (i, 0)),
        out_shape=jax.ShapeDtypeStruct((rows, d), F32),
        compiler_params=_params(("parallel",)),
        name="ffn",
    )(h2d, _operand(nw), _operand(wgu), _operand(wd))


def _merge_ffn_body(h_ref, oa_ref, ob_ref, g_ref, wpa_ref, wpr_ref, wo_ref,
                    nw_ref, wgu_ref, wd_ref, fnw_ref, o_ref, *, final):
    d = h_ref.shape[1]
    groups = _row_groups(h_ref.shape[0])
    yas = [jnp.dot(oa_ref[g, :], wpa_ref[...], preferred_element_type=F32) for g in groups]
    ybs = [jnp.dot(ob_ref[g, :], wpr_ref[...], preferred_element_type=F32) for g in groups]
    merged = [(g_ref[g, :d].astype(F32) * ya + g_ref[g, d:].astype(F32) * yb).astype(BF16)
              for g, ya, yb in zip(groups, yas, ybs)]
    hs = [h_ref[g, :] + jnp.dot(m, wo_ref[...], preferred_element_type=F32) for g, m in zip(groups, merged)]
    h = _ffn_value(hs, nw_ref[...], wgu_ref, wd_ref)
    if final:
        h = _rms(h, fnw_ref[...])
    o_ref[...] = h


def _merge_ffn_call(h2d, oa, ob, gates, wpa, wpr, wo, nw, wgu, wd, fnw, final):
    rows, d = h2d.shape
    row_spec = lambda a: pl.BlockSpec((ROW_TILE, a.shape[1]), lambda i: (i, 0))
    consts = (wpa, wpr, wo, nw, wgu, wd, fnw)
    return pl.pallas_call(
        functools.partial(_merge_ffn_body, final=final),
        grid=(rows // ROW_TILE,),
        in_specs=[row_spec(h2d), row_spec(oa), row_spec(ob), row_spec(gates)]
                 + [_const_spec(c) for c in consts],
        out_specs=pl.BlockSpec((ROW_TILE, d), lambda i: (i, 0)),
        out_shape=jax.ShapeDtypeStruct((rows, d), F32),
        compiler_params=_params(("parallel",)),
        name="merge_ffn",
    )(h2d, oa, ob, gates, *map(_operand, consts))


def _merge_ffn_out_call(h3d, oa, ob, gates, wpa, wpr, wo, nw, wgu, wd, fnw, first, count):
    b, _, d = h3d.shape
    row_spec = lambda a: pl.BlockSpec((pl.Element(1), pl.Element(ROW_TILE), pl.Element(a.shape[2])),
                                      lambda bi, i: (bi, pl.multiple_of(first + i * ROW_TILE, math.gcd(first, ROW_TILE)), 0))
    consts = (wpa, wpr, wo, nw, wgu, wd, fnw)

    def body(h_ref, oa_ref, ob_ref, g_ref, *rest):
        _merge_ffn_body(h_ref.at[0], oa_ref.at[0], ob_ref.at[0], g_ref.at[0], *rest, final=True)

    return pl.pallas_call(
        body,
        grid=(b, count // ROW_TILE),
        in_specs=[row_spec(h3d), row_spec(oa), row_spec(ob), row_spec(gates)]
                 + [_const_spec(c) for c in consts],
        out_specs=pl.BlockSpec((None, ROW_TILE, d), lambda bi, i: (bi, i, 0)),
        out_shape=jax.ShapeDtypeStruct((b, count, d), F32),
        compiler_params=_params(("parallel", "parallel")),
        name="merge_ffn_out",
    )(h3d, oa, ob, gates, *map(_operand, consts))


def _mix_in_body(h_ref, nw_ref, wu_ref, wb_ref, qn_ref, kvn_ref, wuqt_ref, wk_ref, wvt_ref,
                 cqt_ref, sqt_ref, ck_ref, sk_ref,
                 qt_ref, k_ref, vt_ref, hh_ref, g_ref, *, q_lora, kv_lora):
    o_kv = q_lora
    o_pe = o_kv + kv_lora
    o_gt = hh_ref.shape[2]
    u = _rms(h_ref[0], nw_ref[...]).astype(BF16)

    cq = jnp.dot(u, wu_ref[:, :o_kv], preferred_element_type=F32)
    ckv = jnp.dot(u, wu_ref[:, o_kv:o_pe], preferred_element_type=F32)
    kpe = jnp.dot(u, wu_ref[:, o_pe:], preferred_element_type=F32)
    cqn = _rms(cq, qn_ref[...]).astype(BF16)
    ckvn = _rms(ckv, kvn_ref[...]).astype(BF16)

    hh_ref[0] = jnp.dot(u, wb_ref[:, :o_gt], preferred_element_type=F32)
    g_ref[0] = _sigmoid(jnp.dot(u, wb_ref[:, o_gt:], preferred_element_type=F32)).astype(BF16)

    qft = lax.dot_general(wuqt_ref[...], cqn, _NT, preferred_element_type=F32)
    cq_t, sq_t = cqt_ref[...], sqt_ref[...]
    half = QK_ROPE // 2
    for hd in range(MLA_HEADS):
        blk = qft[hd * LANES:(hd + 1) * LANES]
        x1, x2 = blk[QK_NOPE:QK_NOPE + half], blk[QK_NOPE + half:QK_NOPE + QK_ROPE]
        qh = jnp.concatenate([blk[:QK_NOPE] * Q_SCALE, x1 * cq_t - x2 * sq_t, x2 * cq_t + x1 * sq_t,
                              blk[QK_NOPE + QK_ROPE:]], axis=0).astype(BF16)
        for jj in range(qh.shape[1] // KV_BLOCK):
            qt_ref[0, hd, jj] = qh[:, jj * KV_BLOCK:(jj + 1) * KV_BLOCK]

    krot = kpe[:, :LANES] * ck_ref[...] + kpe[:, LANES:] * sk_ref[...]
    kf = jnp.dot(ckvn, wk_ref[...], preferred_element_type=F32)
    for hd in range(MLA_HEADS):
        sl = slice(hd * LANES, (hd + 1) * LANES)
        k_ref[0, hd] = (kf[:, sl] + krot).astype(BF16)
    vt = lax.dot_general(wvt_ref[...], ckvn, _NT, preferred_element_type=F32)
    ones_row = (lax.broadcasted_iota(jnp.int32, (vt.shape[0], 1), 0) % VT_ROWS == V_HEAD).astype(F32)
    vt = (vt + ones_row).astype(BF16)
    for hd in range(MLA_HEADS):
        for jj in range(vt.shape[1] // KV_BLOCK):
            vt_ref[0, hd, jj] = vt[hd * VT_ROWS:(hd + 1) * VT_ROWS, jj * KV_BLOCK:(jj + 1) * KV_BLOCK]


def _mix_in_call(h3d, nw, wu, wb, qn, kvn, wuqt, wk, wvt, tabs, hg_cols):
    b, lp, d = h3d.shape
    tm = MIX_TILE
    head_shape = jax.ShapeDtypeStruct((b, MLA_HEADS, lp, LANES), BF16)
    head_spec = pl.BlockSpec((1, MLA_HEADS, tm, LANES), lambda bi, i: (bi, 0, i, 0))
    tab_spec = pl.BlockSpec((tm, LANES), lambda bi, i: (i, 0))
    tabt_spec = pl.BlockSpec((QK_ROPE // 2, tm), lambda bi, i: (0, i))
    qt_spec = pl.BlockSpec((1, MLA_HEADS, tm // KV_BLOCK, LANES, KV_BLOCK), lambda bi, i: (bi, 0, i, 0, 0))
    gate_cols = wb.shape[1] - hg_cols
    return pl.pallas_call(
        functools.partial(_mix_in_body, q_lora=qn.shape[1], kv_lora=kvn.shape[1]),
        grid=(b, lp // tm),
        in_specs=[pl.BlockSpec((1, tm, d), lambda bi, i: (bi, i, 0))]
                 + [_const_spec(c) for c in (nw, wu, wb, qn, kvn, wuqt, wk, wvt)]
                 + [tabt_spec] * 2 + [tab_spec] * 2,
        out_specs=[qt_spec, head_spec,
                   pl.BlockSpec((1, MLA_HEADS, tm // KV_BLOCK, VT_ROWS, KV_BLOCK),
                                lambda bi, i: (bi, 0, i, 0, 0)),
                   pl.BlockSpec((1, tm, hg_cols), lambda bi, i: (bi, i, 0)),
                   pl.BlockSpec((1, tm, gate_cols), lambda bi, i: (bi, i, 0))],
        out_shape=[jax.ShapeDtypeStruct((b, MLA_HEADS, lp // KV_BLOCK, LANES, KV_BLOCK), BF16), head_shape,
                   jax.ShapeDtypeStruct((b, MLA_HEADS, lp // KV_BLOCK, VT_ROWS, KV_BLOCK), BF16),
                   jax.ShapeDtypeStruct((b, lp, hg_cols), F32),
                   jax.ShapeDtypeStruct((b, lp, gate_cols), BF16)],
        compiler_params=_params(("parallel", "parallel")),
        name="mix_in",
    )(h3d, *map(_operand, (nw, wu, wb, qn, kvn, wuqt, wk, wvt)), *tabs)


def _attn_body(qt_ref, k_ref, vt_ref, o_ref, acc_scr, m_scr, s_scr, mx_scr, fin_scr, *, n_groups):
    n_sub = SEQ_TILE // Q_BLOCK
    n_diag = SEQ_TILE // K_TILE
    vt_per_tile = K_TILE // KV_BLOCK
    q_per_blk = Q_BLOCK // KV_BLOCK
    key_i = lax.broadcasted_iota(jnp.int32, (K_TILE, Q_BLOCK), 0)
    qry_i = lax.broadcasted_iota(jnp.int32, (K_TILE, Q_BLOCK), 1)

    def needed(j, t):
        return j * K_TILE < (t + 1) * Q_BLOCK

    def scores(qb0, kb, slot, diag_j, only=None):
        for hh in range(2):
            k = k_ref[0, hh, pl.ds(pl.multiple_of(kb * K_TILE, K_TILE), K_TILE), :]
            for t in range(n_sub):
                if (diag_j is not None and not needed(diag_j, t)) or (only is not None and only != (hh, t)):
                    continue
                qt = jnp.concatenate([qt_ref[0, hh, qb0 + t * q_per_blk + i] for i in range(q_per_blk)], axis=1)
                st = jnp.dot(k, qt, preferred_element_type=F32)
                if diag_j is not None and (diag_j + 1) * K_TILE > t * Q_BLOCK + 1:
                    st = jnp.where(key_i + diag_j * K_TILE <= qry_i + t * Q_BLOCK, st, NEG_BIG)
                s_scr[slot, hh, t] = st
                mx_scr[slot, hh, t] = jnp.max(st, axis=0, keepdims=True)

    def update(kb, slot, diag_j, only=None):
        for hh in range(2):
            vt = jnp.concatenate([vt_ref[0, hh, kb * vt_per_tile + i] for i in range(vt_per_tile)], axis=1)
            for t in range(n_sub):
                if (diag_j is not None and not needed(diag_j, t)) or (only is not None and only != (hh, t)):
                    continue
                m_old = m_scr[hh, t]
                m_new = jnp.maximum(m_old, mx_scr[slot, hh, t])
                p = jnp.exp2(s_scr[slot, hh, t] - m_new).astype(BF16)
                alpha = jnp.exp2(m_old - m_new)
                acc_scr[hh, t] = alpha * acc_scr[hh, t] + jnp.dot(vt, p, preferred_element_type=F32)
                m_scr[hh, t] = m_new

    def finalize(r0):
        for t in range(n_sub):
            outs = []
            for hh in range(2):
                a = fin_scr[hh, t]
                outs.append((a[:V_HEAD, :] / a[V_HEAD:V_HEAD + 1, :]).T)
            o_ref[0, pl.ds(r0 + t * Q_BLOCK, Q_BLOCK), :] = jnp.concatenate(outs, axis=-1).astype(BF16)

    def group(gi, carry):
        r0 = pl.multiple_of(gi * SEQ_TILE, SEQ_TILE)
        qb0 = gi * (SEQ_TILE // KV_BLOCK)
        n_full = gi * n_diag
        chains = [(hh, t) for hh in range(2) for t in range(n_sub)]
        acc_scr[...] = jnp.zeros(acc_scr.shape, F32)
        m_scr[...] = jnp.full(m_scr.shape, NEG_BIG, F32)

        diag_order = list(range(n_diag - 1, -1, -1))
        scores(qb0, n_full + diag_order[0], 0, diag_order[0])
        finalize(pl.multiple_of(jnp.maximum(gi - 1, 0) * SEQ_TILE, SEQ_TILE))
        for pos in range(1, n_diag):
            scores(qb0, n_full + diag_order[pos], pos, diag_order[pos])
            update(n_full + diag_order[pos - 1], pos - 1, diag_order[pos - 1])

        def kv_trip(i):
            kb = i * n_diag
            for c in chains:
                scores(qb0, kb, 0, None, c)
                update(jnp.where(i == 0, n_full, kb - 1), n_diag - 1, None, c)
            for j in range(1, n_diag):
                for c in chains:
                    scores(qb0, kb + j, j, None, c)
                    update(kb + j - 1, j - 1, None, c)

        def kv_trips(ii, c):
            for u in range(ATTN_UNROLL):
                kv_trip(ii * ATTN_UNROLL + u)
            return c

        lax.fori_loop(0, gi // ATTN_UNROLL, kv_trips, 0)
        for u in range(ATTN_UNROLL - 1):
            @pl.when(gi % ATTN_UNROLL > u)
            def _():
                kv_trip((gi // ATTN_UNROLL) * ATTN_UNROLL + u)
        update(jnp.where(gi == 0, 0, n_full - 1), n_diag - 1, None)
        fin_scr[...] = acc_scr[...]
        return carry

    fin_scr[...] = jnp.ones(fin_scr.shape, F32)
    lax.fori_loop(0, n_groups, group, 0)
    finalize((n_groups - 1) * SEQ_TILE)


def _attn_call(qt, k, vt):
    b, nh, lp, _ = k.shape
    k_spec = pl.BlockSpec((1, 2, lp, LANES), lambda bi, hp: (bi, hp, 0, 0))
    blk_spec = lambda a: pl.BlockSpec((1, 2) + a.shape[2:], lambda bi, hp: (bi, hp, 0, 0, 0))
    n_sub = SEQ_TILE // Q_BLOCK
    return pl.pallas_call(
        functools.partial(_attn_body, n_groups=lp // SEQ_TILE),
        grid=(b, nh // 2),
        in_specs=[blk_spec(qt), k_spec, blk_spec(vt)],
        out_specs=pl.BlockSpec((1, lp, 2 * V_HEAD), lambda bi, hp: (bi, 0, hp)),
        out_shape=jax.ShapeDtypeStruct((b, lp, nh * V_HEAD), BF16),
        scratch_shapes=[pltpu.VMEM((2, n_sub, VT_ROWS, Q_BLOCK), F32),
                        pltpu.VMEM((2, n_sub, 1, Q_BLOCK), F32),
                        pltpu.VMEM((SEQ_TILE // K_TILE, 2, n_sub, K_TILE, Q_BLOCK), F32),
                        pltpu.VMEM((SEQ_TILE // K_TILE, 2, n_sub, 1, Q_BLOCK), F32),
                        pltpu.VMEM((2, n_sub, VT_ROWS, Q_BLOCK), F32)],
        compiler_params=_params(("parallel", "parallel")),
        name="attn",
    )(qt, k, vt)


def _split3_bf16(x):
    hi = x.astype(BF16)
    r1 = x - hi.astype(F32)
    mid = r1.astype(BF16)
    lo = (r1 - mid.astype(F32)).astype(BF16)
    return hi, mid, lo


def _hgrn_levels():
    w, out = HG_CHUNK // 2, []
    while w >= 1:
        out.append(w)
        w //= 2
    return out


def _hgrn_sum_matrix():
    c = HG_CHUNK
    t = np.arange(c)[:, None]
    s = np.arange(c)[None, :]
    blocks = [s <= t, s > t]
    for w in _hgrn_levels():
        r = (t // (2 * w)) * (2 * w) + w - 1
        is_q = (t // w) % 2 == 1
        blocks.append(np.where(is_q, (s > r) & (s <= t), (s > t) & (s <= r)))
    m = np.concatenate(blocks, axis=0).astype(np.float32)
    return jnp.asarray(np.concatenate([m, m, m], axis=1), dtype=BF16)


def _hgrn_body(hh_ref, lbraw_ref, nw_ref, p3_ref, o_ref, st_scr, *, layer):
    c = HG_CHUNK
    hw = HG_HEADS * HG_DIM
    n_chunks = hh_ref.shape[1] // c
    levels = _hgrn_levels()

    @pl.when(pl.program_id(1) == 0)
    def _():
        st_scr[...] = jnp.zeros(st_scr.shape, F32)

    raw = lbraw_ref[...]
    depth = raw.shape[0]
    rows = [raw[j:j + 1, :] for j in range(depth)]
    mx = functools.reduce(jnp.maximum, rows)
    ex = [jnp.exp(r - mx) for r in rows]
    inv_den = 1.0 / functools.reduce(jnp.add, ex)
    lb = jnp.zeros_like(mx)
    for j in range(1, layer + 1):
        lb = lb + ex[j] * inv_den
    one_m_lb = 1.0 - lb
    nw = nw_ref[...]
    scale = HG_DIM ** -0.5

    ri = lax.broadcasted_iota(jnp.int32, (c, c), 0)
    ci = lax.broadcasted_iota(jnp.int32, (c, c), 1)
    rowc = lax.broadcasted_iota(jnp.int32, (c, 1), 0)
    diag = ri == ci
    is_q = [(rowc // w) % 2 == 1 for w in levels]
    same_parent = [(ri // (2 * w)) == (ci // (2 * w)) for w in levels]

    def prepare(r0):
        xq = hh_ref[0, pl.ds(r0, c), 0:hw]
        z = hh_ref[0, pl.ds(r0, c), hw:2 * hw]
        vin = hh_ref[0, pl.ds(r0, c), 2 * hw:3 * hw]

        e = jnp.exp(-jnp.abs(z))
        r = 1.0 / (1.0 + e)
        er = e * r
        pos = z >= 0
        f = lb + one_m_lb * jnp.where(pos, r, er)
        logf2 = jnp.log(jnp.maximum(f, F_MIN)) * LOG2E
        kk = one_m_lb * jnp.where(pos, er, r)
        qq = xq * _sigmoid(xq) * scale
        expo = jnp.dot(p3_ref[...], jnp.concatenate(_split3_bf16(logf2), axis=0),
                       preferred_element_type=F32)
        return qq, kk, vin.astype(BF16), jnp.exp2(expo)

    def products(qq, kk, vb, dec):
        heads = []
        for hd in range(HG_HEADS):
            sl = slice(hd * HG_DIM, (hd + 1) * HG_DIM)
            qh, kh, vh = qq[:, sl], kk[:, sl], vb[:, sl]
            blk = lambda i: dec[i * c:(i + 1) * c, sl]
            st = st_scr[hd]
            d_cum = blk(0)
            parts = [lax.dot_general(qh.astype(BF16), kh.astype(BF16), _NT, preferred_element_type=F32)]
            for i, w in enumerate(levels):
                d_w = blk(2 + i)
                qt = jnp.where(is_q[i], qh * d_w, 0.0).astype(BF16)
                kt = jnp.where(is_q[i], 0.0, kh * d_w).astype(BF16)
                parts.append(lax.dot_general(qt, kt, _NT, preferred_element_type=F32))
            qe = (qh * d_cum).astype(BF16)
            o_st = lax.dot_general(qe, st.astype(BF16), _NT, preferred_element_type=F32)
            kd = (kh * blk(1)).astype(BF16)
            st_scr[hd] = st * d_cum[c - 1:c, :] + lax.dot_general(vh, kd, _TN, preferred_element_type=F32)
            heads.append((sl, vh, parts, o_st))
        return heads

    def finish(r0, heads):
        for sl, vh, parts, o_st in heads:
            a = jnp.where(diag, parts[0], 0.0)
            for i, w in enumerate(levels):
                a = a + (parts[1 + i] if 2 * w == c else jnp.where(same_parent[i], parts[1 + i], 0.0))
            o_acc = o_st + jnp.dot(a.astype(BF16), vh, preferred_element_type=F32)
            gh = hh_ref[0, pl.ds(r0, c), 3 * hw + sl.start:3 * hw + sl.stop]
            y = _rms(o_acc, nw) * (gh * _sigmoid(gh))
            o_ref[0, pl.ds(r0, c), sl] = y.astype(BF16)

    def chunks(it, carry):
        starts = [pl.multiple_of((it * HG_UNROLL + j) * c, c) for j in range(HG_UNROLL)]
        prepared = [prepare(r0) for r0 in starts]
        produced = [products(*pr) for pr in prepared]
        for r0, heads in zip(starts, produced):
            finish(r0, heads)
        return carry

    lax.fori_loop(0, n_chunks // HG_UNROLL, chunks, 0)


def _hgrn_call(hh, lb_raw, nw, layer):
    b, lp, cols = hh.shape
    hw = HG_HEADS * HG_DIM
    t = SEQ_TILE
    p3 = _hgrn_sum_matrix()
    return pl.pallas_call(
        functools.partial(_hgrn_body, layer=layer),
        grid=(b, lp // t),
        in_specs=[pl.BlockSpec((1, t, cols), lambda bi, i: (bi, i, 0)),
                  _const_spec(lb_raw), _const_spec(nw), _const_spec(p3)],
        out_specs=pl.BlockSpec((1, t, hw), lambda bi, i: (bi, i, 0)),
        out_shape=jax.ShapeDtypeStruct((b, lp, hw), BF16),
        scratch_shapes=[pltpu.VMEM((HG_HEADS, HG_DIM, HG_DIM), F32)],
        compiler_params=_params(("parallel", "arbitrary")),
        name="hgrn2",
    )(hh, lb_raw, _operand(nw), p3)


def _rope_tables(lp):
    half = QK_ROPE // 2
    inv = ROPE_THETA ** (-jnp.arange(half, dtype=F32) / half)
    ang = jnp.arange(lp).astype(F32)[:, None] * inv[None, :]
    cos, sin = jnp.cos(ang), jnp.sin(ang)
    zpad = jnp.zeros((lp, LANES - QK_NOPE - QK_ROPE), F32)
    sin_s = jnp.concatenate([jnp.zeros((lp, QK_NOPE), F32), -sin, sin, zpad], axis=1)
    cos_k = jnp.concatenate([jnp.zeros((lp, QK_NOPE), F32), cos, cos, zpad], axis=1)
    return (cos * Q_SCALE).T, (sin * Q_SCALE).T, cos_k, sin_s


def _swap_halves(w):
    half = w.shape[-1] // 2
    return jnp.concatenate([w[..., half:], w[..., :half]], axis=-1)


def _prep_mixer_weights(w_in, w_uq, w_ukv, q_lora, kv_lora):
    depth, d, _ = w_in.shape
    hg = HG_HEADS * HG_DIM
    o_pe = q_lora + kv_lora
    o_hg = o_pe + QK_ROPE
    w_pe = w_in[:, :, o_pe:o_hg]
    zl = jnp.zeros((depth, d, QK_NOPE), F32)
    zr = jnp.zeros((depth, d, LANES - QK_NOPE - QK_ROPE), F32)
    wu = jnp.concatenate([w_in[:, :, :o_pe], zl, w_pe, zr, zl, _swap_halves(w_pe), zr], axis=2).astype(BF16)
    wb = w_in[:, :, o_hg:].astype(BF16)

    dqk = QK_NOPE + QK_ROPE
    uq = w_uq.reshape(depth, q_lora, MLA_HEADS, dqk)
    zq = jnp.zeros((depth, q_lora, MLA_HEADS, LANES - dqk), F32)
    wuqt = jnp.swapaxes(jnp.concatenate([uq, zq], axis=3).reshape(depth, q_lora, -1), 1, 2).astype(BF16)

    ukv = w_ukv.reshape(depth, kv_lora, MLA_HEADS, QK_NOPE + V_HEAD)
    zk = jnp.zeros((depth, kv_lora, MLA_HEADS, LANES - QK_NOPE), F32)
    zv = jnp.zeros((depth, kv_lora, MLA_HEADS, VT_ROWS - V_HEAD), F32)
    wk = jnp.concatenate([ukv[..., :QK_NOPE], zk], axis=3).reshape(depth, kv_lora, -1).astype(BF16)
    wv = jnp.concatenate([ukv[..., QK_NOPE:], zv], axis=3).reshape(depth, kv_lora, -1)
    wvt = jnp.swapaxes(wv, 1, 2).astype(BF16)
    return wu, wb, wuqt, wk, wvt, 4 * hg


def kernel(x, meta_tokens, ffn1_norm, ffn1_w_gu, ffn1_w_down, mix_norm, w_in, q_norm, kv_norm,
           w_uq, w_ukv, hg_lb_raw, hg_norm, w_proj_attn, w_proj_rec, w_out,
           ffn2_norm, ffn2_w_gu, ffn2_w_down, final_norm):
    b, seq, d = x.shape
    depth = w_in.shape[0]
    q_lora, kv_lora = q_norm.shape[1], kv_norm.shape[1]
    l = N_META + seq
    lp = -(-l // SEQ_TILE) * SEQ_TILE
    assert (b * lp) % ROW_TILE == 0 and lp % MIX_TILE == 0

    meta = jnp.broadcast_to(meta_tokens.astype(x.dtype)[None], (b, N_META, d))
    h = jnp.concatenate([meta, x, jnp.zeros((b, lp - l, d), x.dtype)], axis=1)

    wgu1, wgu2 = ffn1_w_gu.astype(BF16), ffn2_w_gu.astype(BF16)
    wd1, wd2 = (0.5 * ffn1_w_down).astype(BF16), (0.5 * ffn2_w_down).astype(BF16)
    wpa, wpr, wo = w_proj_attn.astype(BF16), w_proj_rec.astype(BF16), w_out.astype(BF16)
    wu, wb, wuqt, wk, wvt, hg_cols = _prep_mixer_weights(w_in, w_uq, w_ukv, q_lora, kv_lora)
    tabs = _rope_tables(lp)
    row2 = lambda a: a[:, None, :]
    n1, nm, n2, qn, kvn, hgn = map(row2, (ffn1_norm, mix_norm, ffn2_norm, q_norm, kv_norm, hg_norm))
    fnw = final_norm[None, :]
    lb_raw = hg_lb_raw.astype(F32)

    h2 = h.reshape(b * lp, d)
    for layer in range(depth):
        at = lambda a: _Layer(a, layer)
        h2 = _ffn_call(h2, at(n1), at(wgu1), at(wd1))
        qt, k, vt, hh, gates = _mix_in_call(h2.reshape(b, lp, d), at(nm), at(wu), at(wb), at(qn), at(kvn),
                                            at(wuqt), at(wk), at(wvt), tabs, hg_cols)
        o_attn = _attn_call(qt, k, vt)
        o_rec = _hgrn_call(hh, lb_raw, at(hgn), layer)
        tail = (at(wpa), at(wpr), at(wo), at(n2), at(wgu2), at(wd2), fnw)
        if layer == depth - 1 and seq % ROW_TILE == 0:
            return _merge_ffn_out_call(h2.reshape(b, lp, d), o_attn, o_rec, gates, *tail,
                                       first=N_META, count=seq)
        h2 = _merge_ffn_call(h2, o_attn.reshape(b * lp, -1), o_rec.reshape(b * lp, -1),
                             gates.reshape(b * lp, -1), *tail, final=(layer == depth - 1))
    return h2.reshape(b, lp, d)[:, N_META:l]
```

```python
import functools
import math
from typing import NamedTuple

import numpy as np
import jax
import jax.numpy as jnp
from jax import lax
from jax.experimental import pallas as pl
from jax.experimental.pallas import tpu as pltpu

F32 = jnp.float32
BF16 = jnp.bfloat16

N_META = 16
MLA_HEADS = 8
QK_NOPE = 64
QK_ROPE = 32
V_HEAD = 64
ROPE_THETA = 10000.0
HG_HEADS = 4
HG_DIM = 128
EPS = 1e-6
F_MIN = 1e-20
NEG_BIG = -1e30
LOG2E = math.log2(math.e)
Q_SCALE = LOG2E * (QK_NOPE + QK_ROPE) ** -0.5

LANES = 128
SUBLANES = 8
MXU_TILE = 256
VMEM_LIMIT_BYTES = 56 * 1024 * 1024

SEQ_TILE = 768
Q_BLOCK = 256
K_TILE = 256
KV_BLOCK = 128
ATTN_UNROLL = 2
VT_ROWS = 80
S_PAD_ROWS = 8
MIX_TILE = 384
ROW_TILE = 512
HG_CHUNK = 64
HG_UNROLL = 2
FFN_CHUNKS = 2
FFN_SPLIT = 2

_NT = (((1,), (1,)), ((), ()))
_TN = (((0,), (0,)), ((), ()))


def _sigmoid(x):
    return 1.0 / (1.0 + jnp.exp(-x))


def _rms(x, w):
    return x * lax.rsqrt(jnp.mean(x * x, axis=-1, keepdims=True) + EPS) * w


class _Layer(NamedTuple):
    stacked: jax.Array
    layer: int

    @property
    def shape(self):
        return self.stacked.shape[1:]


def _operand(p):
    return p.stacked if isinstance(p, _Layer) else p


def _const_spec(p):
    nd = len(p.shape)
    if isinstance(p, _Layer):
        return pl.BlockSpec((None,) + tuple(p.shape), lambda *_: (p.layer,) + (0,) * nd,
                            pipeline_mode=pl.Buffered(1))
    return pl.BlockSpec(tuple(p.shape), lambda *_: (0,) * nd, pipeline_mode=pl.Buffered(1))


def _params(sem):
    return pltpu.CompilerParams(dimension_semantics=sem, vmem_limit_bytes=VMEM_LIMIT_BYTES)


def _ffn_chunks(d_ff):
    n_tiles = d_ff // MXU_TILE
    bounds = [round(i * n_tiles / FFN_CHUNKS) * MXU_TILE for i in range(FFN_CHUNKS + 1)]
    return list(zip(bounds[:-1], bounds[1:]))


def _row_groups(n_rows):
    rows = n_rows // FFN_SPLIT
    return [slice(i * rows, (i + 1) * rows) for i in range(FFN_SPLIT)]


def _ffn_value(xs, nw, wgu_ref, wd_ref):
    d_ff = wd_ref.shape[0]
    accs = list(xs)
    xns = [_rms(a, nw).astype(BF16) for a in accs]
    for lo, hi in _ffn_chunks(d_ff):
        gs = [jnp.dot(xn, wgu_ref[:, lo:hi], preferred_element_type=F32) for xn in xns]
        us = [jnp.dot(xn, wgu_ref[:, d_ff + lo:d_ff + hi], preferred_element_type=F32) for xn in xns]
        acts = [(g * _sigmoid(g) * u).astype(BF16) for g, u in zip(gs, us)]
        accs = [acc + jnp.dot(a, wd_ref[lo:hi, :], preferred_element_type=F32) for acc, a in zip(accs, acts)]
    return jnp.concatenate(accs, axis=0)


def _ffn_body(x_ref, nw_ref, wgu_ref, wd_ref, o_ref):
    xs = [x_ref[g, :] for g in _row_groups(x_ref.shape[0])]
    o_ref[...] = _ffn_value(xs, nw_ref[...], wgu_ref, wd_ref)


def _ffn_call(h2d, nw, wgu, wd):
    rows, d = h2d.shape
    return pl.pallas_call(
        _ffn_body,
        grid=(rows // ROW_TILE,),
        in_specs=[
            pl.BlockSpec((ROW_TILE, d), lambda i: (i, 0)),
            _const_spec(nw),
            _const_spec(wgu),
            _const_spec(wd),
        ],
        out_specs=pl.BlockSpec((ROW_TILE, d), lambda i: (i, 0)),
        out_shape=jax.ShapeDtypeStruct((rows, d), F32),
        compiler_params=_params(("parallel",)),
        name="ffn",
    )(h2d, _operand(nw), _operand(wgu), _operand(wd))


def _merge_ffn_body(h_ref, oa_ref, ob_ref, g_ref, wpa_ref, wpr_ref, wo_ref,
                    nw_ref, wgu_ref, wd_ref, fnw_ref, o_ref, *, final):
    d = h_ref.shape[1]
    groups = _row_groups(h_ref.shape[0])
    yas = [jnp.dot(oa_ref[g, :], wpa_ref[...], preferred_element_type=F32) for g in groups]
    ybs = [jnp.dot(ob_ref[g, :], wpr_ref[...], preferred_element_type=F32) for g in groups]
    merged = [(g_ref[g, :d].astype(F32) * ya + g_ref[g, d:].astype(F32) * yb).astype(BF16)
              for g, ya, yb in zip(groups, yas, ybs)]
    hs = [h_ref[g, :] + jnp.dot(m, wo_ref[...], preferred_element_type=F32) for g, m in zip(groups, merged)]
    h = _ffn_value(hs, nw_ref[...], wgu_ref, wd_ref)
    if final:
        h = _rms(h, fnw_ref[...])
    o_ref[...] = h


def _merge_ffn_call(h2d, oa, ob, gates, wpa, wpr, wo, nw, wgu, wd, fnw, final):
    rows, d = h2d.shape
    row_spec = lambda a: pl.BlockSpec((ROW_TILE, a.shape[1]), lambda i: (i, 0))
    consts = (wpa, wpr, wo, nw, wgu, wd, fnw)
    return pl.pallas_call(
        functools.partial(_merge_ffn_body, final=final),
        grid=(rows // ROW_TILE,),
        in_specs=[row_spec(h2d), row_spec(oa), row_spec(ob), row_spec(gates)]
                 + [_const_spec(c) for c in consts],
        out_specs=pl.BlockSpec((ROW_TILE, d), lambda i: (i, 0)),
        out_shape=jax.ShapeDtypeStruct((rows, d), F32),
        compiler_params=_params(("parallel",)),
        name="merge_ffn",
    )(h2d, oa, ob, gates, *map(_operand, consts))


def _merge_ffn_out_call(h3d, oa, ob, gates, wpa, wpr, wo, nw, wgu, wd, fnw, first, count):
    b, _, d = h3d.shape
    row_spec = lambda a: pl.BlockSpec((pl.Element(1), pl.Element(ROW_TILE), pl.Element(a.shape[2])),
                                      lambda bi, i: (bi, pl.multiple_of(first + i * ROW_TILE, math.gcd(first, ROW_TILE)), 0))
    consts = (wpa, wpr, wo, nw, wgu, wd, fnw)

    def body(h_ref, oa_ref, ob_ref, g_ref, *rest):
        _merge_ffn_body(h_ref.at[0], oa_ref.at[0], ob_ref.at[0], g_ref.at[0], *rest, final=True)

    return pl.pallas_call(
        body,
        grid=(b, count // ROW_TILE),
        in_specs=[row_spec(h3d), row_spec(oa), row_spec(ob), row_spec(gates)]
                 + [_const_spec(c) for c in consts],
        out_specs=pl.BlockSpec((None, ROW_TILE, d), lambda bi, i: (bi, i, 0)),
        out_shape=jax.ShapeDtypeStruct((b, count, d), F32),
        compiler_params=_params(("parallel", "parallel")),
        name="merge_ffn_out",
    )(h3d, oa, ob, gates, *map(_operand, consts))


def _mix_in_body(h_ref, nw_ref, wu_ref, wb_ref, qn_ref, kvn_ref, wuqt_ref, wk_ref, wvt_ref,
                 cqt_ref, sqt_ref, ck_ref, sk_ref,
                 qt_ref, k_ref, vt_ref, hh_ref, g_ref, *, q_lora, kv_lora):
    o_kv = q_lora
    o_pe = o_kv + kv_lora
    o_gt = hh_ref.shape[2]
    u = _rms(h_ref[0], nw_ref[...]).astype(BF16)

    cq = jnp.dot(u, wu_ref[:, :o_kv], preferred_element_type=F32)
    ckv = jnp.dot(u, wu_ref[:, o_kv:o_pe], preferred_element_type=F32)
    kpe = jnp.dot(u, wu_ref[:, o_pe:], preferred_element_type=F32)
    cqn = _rms(cq, qn_ref[...]).astype(BF16)
    ckvn = _rms(ckv, kvn_ref[...]).astype(BF16)

    hh_ref[0] = jnp.dot(u, wb_ref[:, :o_gt], preferred_element_type=F32)
    g_ref[0] = _sigmoid(jnp.dot(u, wb_ref[:, o_gt:], preferred_element_type=F32)).astype(BF16)

    qft = lax.dot_general(wuqt_ref[...], cqn, _NT, preferred_element_type=F32)
    cq_t, sq_t = cqt_ref[...], sqt_ref[...]
    half = QK_ROPE // 2
    for hd in range(MLA_HEADS):
        blk = qft[hd * LANES:(hd + 1) * LANES]
        x1, x2 = blk[QK_NOPE:QK_NOPE + half], blk[QK_NOPE + half:QK_NOPE + QK_ROPE]
        qh = jnp.concatenate([blk[:QK_NOPE] * Q_SCALE, x1 * cq_t - x2 * sq_t, x2 * cq_t + x1 * sq_t,
                              blk[QK_NOPE + QK_ROPE:]], axis=0).astype(BF16)
        for jj in range(qh.shape[1] // KV_BLOCK):
            qt_ref[0, hd, jj] = qh[:, jj * KV_BLOCK:(jj + 1) * KV_BLOCK]

    krot = kpe[:, :LANES] * ck_ref[...] + kpe[:, LANES:] * sk_ref[...]
    kf = jnp.dot(ckvn, wk_ref[...], preferred_element_type=F32)
    for hd in range(MLA_HEADS):
        sl = slice(hd * LANES, (hd + 1) * LANES)
        k_ref[0, hd] = (kf[:, sl] + krot).astype(BF16)
    vt = lax.dot_general(wvt_ref[...], ckvn, _NT, preferred_element_type=F32)
    ones_row = (lax.broadcasted_iota(jnp.int32, (vt.shape[0], 1), 0) % VT_ROWS == V_HEAD).astype(F32)
    vt = (vt + ones_row).astype(BF16)
    for hd in range(MLA_HEADS):
        for jj in range(vt.shape[1] // KV_BLOCK):
            vt_ref[0, hd, jj] = vt[hd * VT_ROWS:(hd + 1) * VT_ROWS, jj * KV_BLOCK:(jj + 1) * KV_BLOCK]


def _mix_in_call(h3d, nw, wu, wb, qn, kvn, wuqt, wk, wvt, tabs, hg_cols):
    b, lp, d = h3d.shape
    tm = MIX_TILE
    head_shape = jax.ShapeDtypeStruct((b, MLA_HEADS, lp, LANES), BF16)
    head_spec = pl.BlockSpec((1, MLA_HEADS, tm, LANES), lambda bi, i: (bi, 0, i, 0))
    tab_spec = pl.BlockSpec((tm, LANES), lambda bi, i: (i, 0))
    tabt_spec = pl.BlockSpec((QK_ROPE // 2, tm), lambda bi, i: (0, i))
    qt_spec = pl.BlockSpec((1, MLA_HEADS, tm // KV_BLOCK, LANES, KV_BLOCK), lambda bi, i: (bi, 0, i, 0, 0))
    gate_cols = wb.shape[1] - hg_cols
    return pl.pallas_call(
        functools.partial(_mix_in_body, q_lora=qn.shape[1], kv_lora=kvn.shape[1]),
        grid=(b, lp // tm),
        in_specs=[pl.BlockSpec((1, tm, d), lambda bi, i: (bi, i, 0))]
                 + [_const_spec(c) for c in (nw, wu, wb, qn, kvn, wuqt, wk, wvt)]
                 + [tabt_spec] * 2 + [tab_spec] * 2,
        out_specs=[qt_spec, head_spec,
                   pl.BlockSpec((1, MLA_HEADS, tm // KV_BLOCK, VT_ROWS, KV_BLOCK),
                                lambda bi, i: (bi, 0, i, 0, 0)),
                   pl.BlockSpec((1, tm, hg_cols), lambda bi, i: (bi, i, 0)),
                   pl.BlockSpec((1, tm, gate_cols), lambda bi, i: (bi, i, 0))],
        out_shape=[jax.ShapeDtypeStruct((b, MLA_HEADS, lp // KV_BLOCK, LANES, KV_BLOCK), BF16), head_shape,
                   jax.ShapeDtypeStruct((b, MLA_HEADS, lp // KV_BLOCK, VT_ROWS, KV_BLOCK), BF16),
                   jax.ShapeDtypeStruct((b, lp, hg_cols), F32),
                   jax.ShapeDtypeStruct((b, lp, gate_cols), BF16)],
        compiler_params=_params(("parallel", "parallel")),
        name="mix_in",
    )(h3d, *map(_operand, (nw, wu, wb, qn, kvn, wuqt, wk, wvt)), *tabs)


def _attn_body(qt_ref, k_ref, vt_ref, o_ref, acc_scr, m_scr, s_scr, mx_scr, fin_scr, *, n_groups):
    n_sub = SEQ_TILE // Q_BLOCK
    n_diag = SEQ_TILE // K_TILE
    vt_per_tile = K_TILE // KV_BLOCK
    q_per_blk = Q_BLOCK // KV_BLOCK
    key_i = lax.broadcasted_iota(jnp.int32, (K_TILE, Q_BLOCK), 0)
    qry_i = lax.broadcasted_iota(jnp.int32, (K_TILE, Q_BLOCK), 1)

    def needed(j, t):
        return j * K_TILE < (t + 1) * Q_BLOCK

    def scores(qb0, kb, slot, diag_j, only=None):
        for hh in range(2):
            k = k_ref[0, hh, pl.ds(pl.multiple_of(kb * K_TILE, K_TILE), K_TILE), :]
            for t in range(n_sub):
                if (diag_j is not None and not needed(diag_j, t)) or (only is not None and only != (hh, t)):
                    continue
                qt = jnp.concatenate([qt_ref[0, hh, qb0 + t * q_per_blk + i] for i in range(q_per_blk)], axis=1)
                st = jnp.dot(k, qt, preferred_element_type=F32)
                if diag_j is not None and (diag_j + 1) * K_TILE > t * Q_BLOCK + 1:
                    st = jnp.where(key_i + diag_j * K_TILE <= qry_i + t * Q_BLOCK, st, NEG_BIG)
                s_scr[slot, hh, t, :K_TILE, :] = st
                mx_scr[slot, hh, t] = jnp.max(st, axis=0, keepdims=True)

    def update(kb, slot, diag_j, only=None):
        for hh in range(2):
            vt = jnp.concatenate([vt_ref[0, hh, kb * vt_per_tile + i] for i in range(vt_per_tile)], axis=1)
            for t in range(n_sub):
                if (diag_j is not None and not needed(diag_j, t)) or (only is not None and only != (hh, t)):
                    continue
                m_old = m_scr[hh, t]
                m_new = jnp.maximum(m_old, mx_scr[slot, hh, t])
                p = jnp.exp2(s_scr[slot, hh, t, :K_TILE, :] - m_new).astype(BF16)
                alpha = jnp.exp2(m_old - m_new)
                acc_scr[hh, t] = alpha * acc_scr[hh, t] + jnp.dot(vt, p, preferred_element_type=F32)
                m_scr[hh, t] = m_new

    def finalize(r0):
        for t in range(n_sub):
            outs = []
            for hh in range(2):
                a = fin_scr[hh, t]
                outs.append((a[:V_HEAD, :] / a[V_HEAD:V_HEAD + 1, :]).T)
            o_ref[0, pl.ds(r0 + t * Q_BLOCK, Q_BLOCK), :] = jnp.concatenate(outs, axis=-1).astype(BF16)

    def group(gi, carry):
        r0 = pl.multiple_of(gi * SEQ_TILE, SEQ_TILE)
        qb0 = gi * (SEQ_TILE // KV_BLOCK)
        n_full = gi * n_diag
        chains = [(hh, t) for hh in range(2) for t in range(n_sub)]
        acc_scr[...] = jnp.zeros(acc_scr.shape, F32)
        m_scr[...] = jnp.full(m_scr.shape, NEG_BIG, F32)

        diag_order = list(range(n_diag - 1, -1, -1))
        scores(qb0, n_full + diag_order[0], 0, diag_order[0])
        finalize(pl.multiple_of(jnp.maximum(gi - 1, 0) * SEQ_TILE, SEQ_TILE))
        for pos in range(1, n_diag):
            scores(qb0, n_full + diag_order[pos], pos, diag_order[pos])
            update(n_full + diag_order[pos - 1], pos - 1, diag_order[pos - 1])

        def kv_trip(i):
            kb = i * n_diag
            for c in chains:
                scores(qb0, kb, 0, None, c)
                update(jnp.where(i == 0, n_full, kb - 1), n_diag - 1, None, c)
            for j in range(1, n_diag):
                for c in chains:
                    scores(qb0, kb + j, j, None, c)
                    update(kb + j - 1, j - 1, None, c)

        def kv_trips(ii, c):
            for u in range(ATTN_UNROLL):
                kv_trip(ii * ATTN_UNROLL + u)
            return c

        lax.fori_loop(0, gi // ATTN_UNROLL, kv_trips, 0)
        for u in range(ATTN_UNROLL - 1):
            @pl.when(gi % ATTN_UNROLL > u)
            def _():
                kv_trip((gi // ATTN_UNROLL) * ATTN_UNROLL + u)
        update(jnp.where(gi == 0, 0, n_full - 1), n_diag - 1, None)
        fin_scr[...] = acc_scr[...]
        return carry

    fin_scr[...] = jnp.ones(fin_scr.shape, F32)
    lax.fori_loop(0, n_groups, group, 0)
    finalize((n_groups - 1) * SEQ_TILE)


def _attn_call(qt, k, vt):
    b, nh, lp, _ = k.shape
    k_spec = pl.BlockSpec((1, 2, lp, LANES), lambda bi, hp: (bi, hp, 0, 0))
    blk_spec = lambda a: pl.BlockSpec((1, 2) + a.shape[2:], lambda bi, hp: (bi, hp, 0, 0, 0))
    n_sub = SEQ_TILE // Q_BLOCK
    return pl.pallas_call(
        functools.partial(_attn_body, n_groups=lp // SEQ_TILE),
        grid=(b, nh // 2),
        in_specs=[blk_spec(qt), k_spec, blk_spec(vt)],
        out_specs=pl.BlockSpec((1, lp, 2 * V_HEAD), lambda bi, hp: (bi, 0, hp)),
        out_shape=jax.ShapeDtypeStruct((b, lp, nh * V_HEAD), BF16),
        scratch_shapes=[pltpu.VMEM((2, n_sub, VT_ROWS, Q_BLOCK), F32),
                        pltpu.VMEM((2, n_sub, 1, Q_BLOCK), F32),
                        pltpu.VMEM((SEQ_TILE // K_TILE, 2, n_sub, K_TILE + S_PAD_ROWS, Q_BLOCK), F32),
                        pltpu.VMEM((SEQ_TILE // K_TILE, 2, n_sub, 1, Q_BLOCK), F32),
                        pltpu.VMEM((2, n_sub, VT_ROWS, Q_BLOCK), F32)],
        compiler_params=_params(("parallel", "parallel")),
        name="attn",
    )(qt, k, vt)


def _split3_bf16(x):
    hi = x.astype(BF16)
    r1 = x - hi.astype(F32)
    mid = r1.astype(BF16)
    lo = (r1 - mid.astype(F32)).astype(BF16)
    return hi, mid, lo


def _hgrn_levels():
    w, out = HG_CHUNK // 2, []
    while w >= 1:
        out.append(w)
        w //= 2
    return out


def _hgrn_sum_matrix():
    c = HG_CHUNK
    t = np.arange(c)[:, None]
    s = np.arange(c)[None, :]
    blocks = [s <= t, s > t]
    for w in _hgrn_levels():
        r = (t // (2 * w)) * (2 * w) + w - 1
        is_q = (t // w) % 2 == 1
        blocks.append(np.where(is_q, (s > r) & (s <= t), (s > t) & (s <= r)))
    m = np.concatenate(blocks, axis=0).astype(np.float32)
    return jnp.asarray(np.concatenate([m, m, m], axis=1), dtype=BF16)


def _hgrn_body(hh_ref, lbraw_ref, nw_ref, p3_ref, o_ref, st_scr, *, layer):
    c = HG_CHUNK
    hw = HG_HEADS * HG_DIM
    n_chunks = hh_ref.shape[1] // c
    levels = _hgrn_levels()

    @pl.when(pl.program_id(1) == 0)
    def _():
        st_scr[...] = jnp.zeros(st_scr.shape, F32)

    raw = lbraw_ref[...]
    depth = raw.shape[0]
    rows = [raw[j:j + 1, :] for j in range(depth)]
    mx = functools.reduce(jnp.maximum, rows)
    ex = [jnp.exp(r - mx) for r in rows]
    inv_den = 1.0 / functools.reduce(jnp.add, ex)
    lb = jnp.zeros_like(mx)
    for j in range(1, layer + 1):
        lb = lb + ex[j] * inv_den
    one_m_lb = 1.0 - lb
    nw = nw_ref[...]
    scale = HG_DIM ** -0.5

    ri = lax.broadcasted_iota(jnp.int32, (c, c), 0)
    ci = lax.broadcasted_iota(jnp.int32, (c, c), 1)
    rowc = lax.broadcasted_iota(jnp.int32, (c, 1), 0)
    diag = ri == ci
    is_q = [(rowc // w) % 2 == 1 for w in levels]
    same_parent = [(ri // (2 * w)) == (ci // (2 * w)) for w in levels]

    def prepare(r0):
        xq = hh_ref[0, pl.ds(r0, c), 0:hw]
        z = hh_ref[0, pl.ds(r0, c), hw:2 * hw]
        vin = hh_ref[0, pl.ds(r0, c), 2 * hw:3 * hw]

        e = jnp.exp(-jnp.abs(z))
        r = 1.0 / (1.0 + e)
        er = e * r
        pos = z >= 0
        f = lb + one_m_lb * jnp.where(pos, r, er)
        logf2 = jnp.log(jnp.maximum(f, F_MIN)) * LOG2E
        kk = one_m_lb * jnp.where(pos, er, r)
        qq = xq * _sigmoid(xq) * scale
        expo = jnp.dot(p3_ref[...], jnp.concatenate(_split3_bf16(logf2), axis=0),
                       preferred_element_type=F32)
        return qq, kk, vin.astype(BF16), jnp.exp2(expo)

    def products(qq, kk, vb, dec):
        heads = []
        for hd in range(HG_HEADS):
            sl = slice(hd * HG_DIM, (hd + 1) * HG_DIM)
            qh, kh, vh = qq[:, sl], kk[:, sl], vb[:, sl]
            blk = lambda i: dec[i * c:(i + 1) * c, sl]
            st = st_scr[hd]
            d_cum = blk(0)
            parts = [lax.dot_general(qh.astype(BF16), kh.astype(BF16), _NT, preferred_element_type=F32)]
            for i, w in enumerate(levels):
                d_w = blk(2 + i)
                qt = jnp.where(is_q[i], qh * d_w, 0.0).astype(BF16)
                kt = jnp.where(is_q[i], 0.0, kh * d_w).astype(BF16)
                parts.append(lax.dot_general(qt, kt, _NT, preferred_element_type=F32))
            qe = (qh * d_cum).astype(BF16)
            o_st = lax.dot_general(qe, st.astype(BF16), _NT, preferred_element_type=F32)
            kd = (kh * blk(1)).astype(BF16)
            st_scr[hd] = st * d_cum[c - 1:c, :] + lax.dot_general(vh, kd, _TN, preferred_element_type=F32)
            heads.append((sl, vh, parts, o_st))
        return heads

    def finish(r0, heads):
        for sl, vh, parts, o_st in heads:
            a = jnp.where(diag, parts[0], 0.0)
            for i, w in enumerate(levels):
                a = a + (parts[1 + i] if 2 * w == c else jnp.where(same_parent[i], parts[1 + i], 0.0))
            o_acc = o_st + jnp.dot(a.astype(BF16), vh, preferred_element_type=F32)
            gh = hh_ref[0, pl.ds(r0, c), 3 * hw + sl.start:3 * hw + sl.stop]
            y = _rms(o_acc, nw) * (gh * _sigmoid(gh))
            o_ref[0, pl.ds(r0, c), sl] = y.astype(BF16)

    def chunks(it, carry):
        starts = [pl.multiple_of((it * HG_UNROLL + j) * c, c) for j in range(HG_UNROLL)]
        prepared = [prepare(r0) for r0 in starts]
        produced = [products(*pr) for pr in prepared]
        for r0, heads in zip(starts, produced):
            finish(r0, heads)
        return carry

    lax.fori_loop(0, n_chunks // HG_UNROLL, chunks, 0)


def _hgrn_call(hh, lb_raw, nw, layer):
    b, lp, cols = hh.shape
    hw = HG_HEADS * HG_DIM
    t = SEQ_TILE
    p3 = _hgrn_sum_matrix()
    return pl.pallas_call(
        functools.partial(_hgrn_body, layer=layer),
        grid=(b, lp // t),
        in_specs=[pl.BlockSpec((1, t, cols), lambda bi, i: (bi, i, 0)),
                  _const_spec(lb_raw), _const_spec(nw), _const_spec(p3)],
        out_specs=pl.BlockSpec((1, t, hw), lambda bi, i: (bi, i, 0)),
        out_shape=jax.ShapeDtypeStruct((b, lp, hw), BF16),
        scratch_shapes=[pltpu.VMEM((HG_HEADS, HG_DIM, HG_DIM), F32)],
        compiler_params=_params(("parallel", "arbitrary")),
        name="hgrn2",
    )(hh, lb_raw, _operand(nw), p3)


def _rope_tables(lp):
    half = QK_ROPE // 2
    inv = ROPE_THETA ** (-jnp.arange(half, dtype=F32) / half)
    ang = jnp.arange(lp).astype(F32)[:, None] * inv[None, :]
    cos, sin = jnp.cos(ang), jnp.sin(ang)
    zpad = jnp.zeros((lp, LANES - QK_NOPE - QK_ROPE), F32)
    sin_s = jnp.concatenate([jnp.zeros((lp, QK_NOPE), F32), -sin, sin, zpad], axis=1)
    cos_k = jnp.concatenate([jnp.zeros((lp, QK_NOPE), F32), cos, cos, zpad], axis=1)
    return (cos * Q_SCALE).T, (sin * Q_SCALE).T, cos_k, sin_s


def _swap_halves(w):
    half = w.shape[-1] // 2
    return jnp.concatenate([w[..., half:], w[..., :half]], axis=-1)


def _prep_mixer_weights(w_in, w_uq, w_ukv, q_lora, kv_lora):
    depth, d, _ = w_in.shape
    hg = HG_HEADS * HG_DIM
    o_pe = q_lora + kv_lora
    o_hg = o_pe + QK_ROPE
    w_pe = w_in[:, :, o_pe:o_hg]
    zl = jnp.zeros((depth, d, QK_NOPE), F32)
    zr = jnp.zeros((depth, d, LANES - QK_NOPE - QK_ROPE), F32)
    wu = jnp.concatenate([w_in[:, :, :o_pe], zl, w_pe, zr, zl, _swap_halves(w_pe), zr], axis=2).astype(BF16)
    wb = w_in[:, :, o_hg:].astype(BF16)

    dqk = QK_NOPE + QK_ROPE
    uq = w_uq.reshape(depth, q_lora, MLA_HEADS, dqk)
    zq = jnp.zeros((depth, q_lora, MLA_HEADS, LANES - dqk), F32)
    wuqt = jnp.swapaxes(jnp.concatenate([uq, zq], axis=3).reshape(depth, q_lora, -1), 1, 2).astype(BF16)

    ukv = w_ukv.reshape(depth, kv_lora, MLA_HEADS, QK_NOPE + V_HEAD)
    zk = jnp.zeros((depth, kv_lora, MLA_HEADS, LANES - QK_NOPE), F32)
    zv = jnp.zeros((depth, kv_lora, MLA_HEADS, VT_ROWS - V_HEAD), F32)
    wk = jnp.concatenate([ukv[..., :QK_NOPE], zk], axis=3).reshape(depth, kv_lora, -1).astype(BF16)
    wv = jnp.concatenate([ukv[..., QK_NOPE:], zv], axis=3).reshape(depth, kv_lora, -1)
    wvt = jnp.swapaxes(wv, 1, 2).astype(BF16)
    return wu, wb, wuqt, wk, wvt, 4 * hg


def kernel(x, meta_tokens, ffn1_norm, ffn1_w_gu, ffn1_w_down, mix_norm, w_in, q_norm, kv_norm,
           w_uq, w_ukv, hg_lb_raw, hg_norm, w_proj_attn, w_proj_rec, w_out,
           ffn2_norm, ffn2_w_gu, ffn2_w_down, final_norm):
    b, seq, d = x.shape
    depth = w_in.shape[0]
    q_lora, kv_lora = q_norm.shape[1], kv_norm.shape[1]
    l = N_META + seq
    lp = -(-l // SEQ_TILE) * SEQ_TILE
    assert (b * lp) % ROW_TILE == 0 and lp % MIX_TILE == 0

    meta = jnp.broadcast_to(meta_tokens.astype(x.dtype)[None], (b, N_META, d))
    h = jnp.concatenate([meta, x, jnp.zeros((b, lp - l, d), x.dtype)], axis=1)

    wgu1, wgu2 = ffn1_w_gu.astype(BF16), ffn2_w_gu.astype(BF16)
    wd1, wd2 = (0.5 * ffn1_w_down).astype(BF16), (0.5 * ffn2_w_down).astype(BF16)
    wpa, wpr, wo = w_proj_attn.astype(BF16), w_proj_rec.astype(BF16), w_out.astype(BF16)
    wu, wb, wuqt, wk, wvt, hg_cols = _prep_mixer_weights(w_in, w_uq, w_ukv, q_lora, kv_lora)
    tabs = _rope_tables(lp)
    row2 = lambda a: a[:, None, :]
    n1, nm, n2, qn, kvn, hgn = map(row2, (ffn1_norm, mix_norm, ffn2_norm, q_norm, kv_norm, hg_norm))
    fnw = final_norm[None, :]
    lb_raw = hg_lb_raw.astype(F32)

    h2 = h.reshape(b * lp, d)
    for layer in range(depth):
        at = lambda a: _Layer(a, layer)
        h2 = _ffn_call(h2, at(n1), at(wgu1), at(wd1))
        qt, k, vt, hh, gates = _mix_in_call(h2.reshape(b, lp, d), at(nm), at(wu), at(wb), at(qn), at(kvn),
                                            at(wuqt), at(wk), at(wvt), tabs, hg_cols)
        o_attn = _attn_call(qt, k, vt)
        o_rec = _hgrn_call(hh, lb_raw, at(hgn), layer)
        tail = (at(wpa), at(wpr), at(wo), at(n2), at(wgu2), at(wd2), fnw)
        if layer == depth - 1 and seq % ROW_TILE == 0:
            return _merge_ffn_out_call(h2.reshape(b, lp, d), o_attn, o_rec, gates, *tail,
                                       first=N_META, count=seq)
        h2 = _merge_ffn_call(h2, o_attn.reshape(b * lp, -1), o_rec.reshape(b * lp, -1),
                             gates.reshape(b * lp, -1), *tail, final=(layer == depth - 1))
    return h2.reshape(b, lp, d)[:, N_META:l]
```

```python
import functools
import math
from typing import NamedTuple

import numpy as np
import jax
import jax.numpy as jnp
from jax import lax
from jax.experimental import pallas as pl
from jax.experimental.pallas import tpu as pltpu

F32 = jnp.float32
BF16 = jnp.bfloat16

N_META = 16
MLA_HEADS = 8
QK_NOPE = 64
QK_ROPE = 32
V_HEAD = 64
ROPE_THETA = 10000.0
HG_HEADS = 4
HG_DIM = 128
EPS = 1e-6
F_MIN = 1e-20
NEG_BIG = -1e30
LOG2E = math.log2(math.e)
Q_SCALE = LOG2E * (QK_NOPE + QK_ROPE) ** -0.5

LANES = 128
SUBLANES = 8
MXU_TILE = 256
VMEM_LIMIT_BYTES = 56 * 1024 * 1024

SEQ_TILE = 768
Q_BLOCK = 256
K_TILE = 256
KV_BLOCK = 128
ATTN_UNROLL = 3
VT_ROWS = 80
MIX_TILE = 384
ROW_TILE = 512
HG_CHUNK = 64
HG_UNROLL = 2
FFN_CHUNKS = 2
FFN_SPLIT = 2

_NT = (((1,), (1,)), ((), ()))
_TN = (((0,), (0,)), ((), ()))


def _sigmoid(x):
    return 1.0 / (1.0 + jnp.exp(-x))


def _rms(x, w):
    return x * lax.rsqrt(jnp.mean(x * x, axis=-1, keepdims=True) + EPS) * w


class _Layer(NamedTuple):
    stacked: jax.Array
    layer: int

    @property
    def shape(self):
        return self.stacked.shape[1:]


def _operand(p):
    return p.stacked if isinstance(p, _Layer) else p


def _const_spec(p):
    nd = len(p.shape)
    if isinstance(p, _Layer):
        return pl.BlockSpec((None,) + tuple(p.shape), lambda *_: (p.layer,) + (0,) * nd,
                            pipeline_mode=pl.Buffered(1))
    return pl.BlockSpec(tuple(p.shape), lambda *_: (0,) * nd, pipeline_mode=pl.Buffered(1))


def _params(sem):
    return pltpu.CompilerParams(dimension_semantics=sem, vmem_limit_bytes=VMEM_LIMIT_BYTES)


def _ffn_chunks(d_ff):
    n_tiles = d_ff // MXU_TILE
    bounds = [round(i * n_tiles / FFN_CHUNKS) * MXU_TILE for i in range(FFN_CHUNKS + 1)]
    return list(zip(bounds[:-1], bounds[1:]))


def _row_groups(n_rows):
    rows = n_rows // FFN_SPLIT
    return [slice(i * rows, (i + 1) * rows) for i in range(FFN_SPLIT)]


def _ffn_value(xs, nw, wgu_ref, wd_ref):
    d_ff = wd_ref.shape[0]
    accs = list(xs)
    xns = [_rms(a, nw).astype(BF16) for a in accs]
    for lo, hi in _ffn_chunks(d_ff):
        gs = [jnp.dot(xn, wgu_ref[:, lo:hi], preferred_element_type=F32) for xn in xns]
        us = [jnp.dot(xn, wgu_ref[:, d_ff + lo:d_ff + hi], preferred_element_type=F32) for xn in xns]
        acts = [(g * _sigmoid(g) * u).astype(BF16) for g, u in zip(gs, us)]
        accs = [acc + jnp.dot(a, wd_ref[lo:hi, :], preferred_element_type=F32) for acc, a in zip(accs, acts)]
    return jnp.concatenate(accs, axis=0)


def _ffn_body(x_ref, nw_ref, wgu_ref, wd_ref, o_ref):
    xs = [x_ref[g, :] for g in _row_groups(x_ref.shape[0])]
    o_ref[...] = _ffn_value(xs, nw_ref[...], wgu_ref, wd_ref)


def _ffn_call(h2d, nw, wgu, wd):
    rows, d = h2d.shape
    return pl.pallas_call(
        _ffn_body,
        grid=(rows // ROW_TILE,),
        in_specs=[
            pl.BlockSpec((ROW_TILE, d), lambda i: (i, 0)),
            _const_spec(nw),
            _const_spec(wgu),
            _const_spec(wd),
        ],
        out_specs=pl.BlockSpec((ROW_TILE, d), lambda i: (i, 0)),
        out_shape=jax.ShapeDtypeStruct((rows, d), F32),
        compiler_params=_params(("parallel",)),
        name="ffn",
    )(h2d, _operand(nw), _operand(wgu), _operand(wd))


def _merge_ffn_body(h_ref, oa_ref, ob_ref, g_ref, wpa_ref, wpr_ref, wo_ref,
                    nw_ref, wgu_ref, wd_ref, fnw_ref, o_ref, *, final):
    d = h_ref.shape[1]
    groups = _row_groups(h_ref.shape[0])
    yas = [jnp.dot(oa_ref[g, :], wpa_ref[...], preferred_element_type=F32) for g in groups]
    ybs = [jnp.dot(ob_ref[g, :], wpr_ref[...], preferred_element_type=F32) for g in groups]
    merged = [(g_ref[g, :d].astype(F32) * ya + g_ref[g, d:].astype(F32) * yb).astype(BF16)
              for g, ya, yb in zip(groups, yas, ybs)]
    hs = [h_ref[g, :] + jnp.dot(m, wo_ref[...], preferred_element_type=F32) for g, m in zip(groups, merged)]
    h = _ffn_value(hs, nw_ref[...], wgu_ref, wd_ref)
    if final:
        h = _rms(h, fnw_ref[...])
    o_ref[...] = h


def _merge_ffn_call(h2d, oa, ob, gates, wpa, wpr, wo, nw, wgu, wd, fnw, final):
    rows, d = h2d.shape
    row_spec = lambda a: pl.BlockSpec((ROW_TILE, a.shape[1]), lambda i: (i, 0))
    consts = (wpa, wpr, wo, nw, wgu, wd, fnw)
    return pl.pallas_call(
        functools.partial(_merge_ffn_body, final=final),
        grid=(rows // ROW_TILE,),
        in_specs=[row_spec(h2d), row_spec(oa), row_spec(ob), row_spec(gates)]
                 + [_const_spec(c) for c in consts],
        out_specs=pl.BlockSpec((ROW_TILE, d), lambda i: (i, 0)),
        out_shape=jax.ShapeDtypeStruct((rows, d), F32),
        compiler_params=_params(("parallel",)),
        name="merge_ffn",
    )(h2d, oa, ob, gates, *map(_operand, consts))


def _merge_ffn_out_call(h3d, oa, ob, gates, wpa, wpr, wo, nw, wgu, wd, fnw, first, count):
    b, _, d = h3d.shape
    row_spec = lambda a: pl.BlockSpec((pl.Element(1), pl.Element(ROW_TILE), pl.Element(a.shape[2])),
                                      lambda bi, i: (bi, pl.multiple_of(first + i * ROW_TILE, math.gcd(first, ROW_TILE)), 0))
    consts = (wpa, wpr, wo, nw, wgu, wd, fnw)

    def body(h_ref, oa_ref, ob_ref, g_ref, *rest):
        _merge_ffn_body(h_ref.at[0], oa_ref.at[0], ob_ref.at[0], g_ref.at[0], *rest, final=True)

    return pl.pallas_call(
        body,
        grid=(b, count // ROW_TILE),
        in_specs=[row_spec(h3d), row_spec(oa), row_spec(ob), row_spec(gates)]
                 + [_const_spec(c) for c in consts],
        out_specs=pl.BlockSpec((None, ROW_TILE, d), lambda bi, i: (bi, i, 0)),
        out_shape=jax.ShapeDtypeStruct((b, count, d), F32),
        compiler_params=_params(("parallel", "parallel")),
        name="merge_ffn_out",
    )(h3d, oa, ob, gates, *map(_operand, consts))


def _mix_in_body(h_ref, nw_ref, wu_ref, wb_ref, qn_ref, kvn_ref, wuqt_ref, wk_ref, wvt_ref,
                 cqt_ref, sqt_ref, ck_ref, sk_ref,
                 qt_ref, k_ref, vt_ref, hh_ref, g_ref, *, q_lora, kv_lora):
    o_kv = q_lora
    o_pe = o_kv + kv_lora
    o_gt = hh_ref.shape[2]
    u = _rms(h_ref[0], nw_ref[...]).astype(BF16)

    cq = jnp.dot(u, wu_ref[:, :o_kv], preferred_element_type=F32)
    ckv = jnp.dot(u, wu_ref[:, o_kv:o_pe], preferred_element_type=F32)
    kpe = jnp.dot(u, wu_ref[:, o_pe:], preferred_element_type=F32)
    cqn = _rms(cq, qn_ref[...]).astype(BF16)
    ckvn = _rms(ckv, kvn_ref[...]).astype(BF16)

    hg = jnp.dot(u, wb_ref[:, :o_gt], preferred_element_type=F32)
    hw = o_gt // 4
    xq, xg = hg[:, :hw], hg[:, 3 * hw:]
    hh_ref[0, :, :hw] = xq * _sigmoid(xq) * (HG_DIM ** -0.5)
    hh_ref[0, :, hw:3 * hw] = hg[:, hw:3 * hw]
    hh_ref[0, :, 3 * hw:] = xg * _sigmoid(xg)
    g_ref[0] = _sigmoid(jnp.dot(u, wb_ref[:, o_gt:], preferred_element_type=F32)).astype(BF16)

    qft = lax.dot_general(wuqt_ref[...], cqn, _NT, preferred_element_type=F32)
    cq_t, sq_t = cqt_ref[...], sqt_ref[...]
    half = QK_ROPE // 2
    for hd in range(MLA_HEADS):
        blk = qft[hd * LANES:(hd + 1) * LANES]
        x1, x2 = blk[QK_NOPE:QK_NOPE + half], blk[QK_NOPE + half:QK_NOPE + QK_ROPE]
        qh = jnp.concatenate([blk[:QK_NOPE] * Q_SCALE, x1 * cq_t - x2 * sq_t, x2 * cq_t + x1 * sq_t,
                              blk[QK_NOPE + QK_ROPE:]], axis=0).astype(BF16)
        for jj in range(qh.shape[1] // KV_BLOCK):
            qt_ref[0, hd, jj] = qh[:, jj * KV_BLOCK:(jj + 1) * KV_BLOCK]

    krot = kpe[:, :LANES] * ck_ref[...] + kpe[:, LANES:] * sk_ref[...]
    kf = jnp.dot(ckvn, wk_ref[...], preferred_element_type=F32)
    for hd in range(MLA_HEADS):
        sl = slice(hd * LANES, (hd + 1) * LANES)
        k_ref[0, hd] = (kf[:, sl] + krot).astype(BF16)
    vt = lax.dot_general(wvt_ref[...], ckvn, _NT, preferred_element_type=F32)
    ones_row = (lax.broadcasted_iota(jnp.int32, (vt.shape[0], 1), 0) % VT_ROWS == V_HEAD).astype(F32)
    vt = (vt + ones_row).astype(BF16)
    for hd in range(MLA_HEADS):
        for jj in range(vt.shape[1] // KV_BLOCK):
            vt_ref[0, hd, jj] = vt[hd * VT_ROWS:(hd + 1) * VT_ROWS, jj * KV_BLOCK:(jj + 1) * KV_BLOCK]


def _mix_in_call(h3d, nw, wu, wb, qn, kvn, wuqt, wk, wvt, tabs, hg_cols):
    b, lp, d = h3d.shape
    tm = MIX_TILE
    head_shape = jax.ShapeDtypeStruct((b, MLA_HEADS, lp, LANES), BF16)
    head_spec = pl.BlockSpec((1, MLA_HEADS, tm, LANES), lambda bi, i: (bi, 0, i, 0))
    tab_spec = pl.BlockSpec((tm, LANES), lambda bi, i: (i, 0))
    tabt_spec = pl.BlockSpec((QK_ROPE // 2, tm), lambda bi, i: (0, i))
    qt_spec = pl.BlockSpec((1, MLA_HEADS, tm // KV_BLOCK, LANES, KV_BLOCK), lambda bi, i: (bi, 0, i, 0, 0))
    gate_cols = wb.shape[1] - hg_cols
    return pl.pallas_call(
        functools.partial(_mix_in_body, q_lora=qn.shape[1], kv_lora=kvn.shape[1]),
        grid=(b, lp // tm),
        in_specs=[pl.BlockSpec((1, tm, d), lambda bi, i: (bi, i, 0))]
                 + [_const_spec(c) for c in (nw, wu, wb, qn, kvn, wuqt, wk, wvt)]
                 + [tabt_spec] * 2 + [tab_spec] * 2,
        out_specs=[qt_spec, head_spec,
                   pl.BlockSpec((1, MLA_HEADS, tm // KV_BLOCK, VT_ROWS, KV_BLOCK),
                                lambda bi, i: (bi, 0, i, 0, 0)),
                   pl.BlockSpec((1, tm, hg_cols), lambda bi, i: (bi, i, 0)),
                   pl.BlockSpec((1, tm, gate_cols), lambda bi, i: (bi, i, 0))],
        out_shape=[jax.ShapeDtypeStruct((b, MLA_HEADS, lp // KV_BLOCK, LANES, KV_BLOCK), BF16), head_shape,
                   jax.ShapeDtypeStruct((b, MLA_HEADS, lp // KV_BLOCK, VT_ROWS, KV_BLOCK), BF16),
                   jax.ShapeDtypeStruct((b, lp, hg_cols), F32),
                   jax.ShapeDtypeStruct((b, lp, gate_cols), BF16)],
        compiler_params=_params(("parallel", "parallel")),
        name="mix_in",
    )(h3d, *map(_operand, (nw, wu, wb, qn, kvn, wuqt, wk, wvt)), *tabs)


def _attn_body(qt_ref, k_ref, vt_ref, o_ref, acc_scr, m_scr, s_scr, mx_scr, fin_scr, *, n_groups):
    n_sub = SEQ_TILE // Q_BLOCK
    n_diag = SEQ_TILE // K_TILE
    vt_per_tile = K_TILE // KV_BLOCK
    q_per_blk = Q_BLOCK // KV_BLOCK
    key_i = lax.broadcasted_iota(jnp.int32, (K_TILE, Q_BLOCK), 0)
    qry_i = lax.broadcasted_iota(jnp.int32, (K_TILE, Q_BLOCK), 1)

    def needed(j, t):
        return j * K_TILE < (t + 1) * Q_BLOCK

    def scores(qb0, kb, slot, diag_j, only=None):
        for hh in range(2):
            k = k_ref[0, hh, pl.ds(pl.multiple_of(kb * K_TILE, K_TILE), K_TILE), :]
            for t in range(n_sub):
                if (diag_j is not None and not needed(diag_j, t)) or (only is not None and only != (hh, t)):
                    continue
                qt = jnp.concatenate([qt_ref[0, hh, qb0 + t * q_per_blk + i] for i in range(q_per_blk)], axis=1)
                st = jnp.dot(k, qt, preferred_element_type=F32)
                if diag_j is not None and (diag_j + 1) * K_TILE > t * Q_BLOCK + 1:
                    st = jnp.where(key_i + diag_j * K_TILE <= qry_i + t * Q_BLOCK, st, NEG_BIG)
                s_scr[slot, hh, t] = st
                mx_scr[slot, hh, t] = jnp.max(st, axis=0, keepdims=True)

    def update(kb, slot, diag_j, only=None):
        for hh in range(2):
            vt = jnp.concatenate([vt_ref[0, hh, kb * vt_per_tile + i] for i in range(vt_per_tile)], axis=1)
            for t in range(n_sub):
                if (diag_j is not None and not needed(diag_j, t)) or (only is not None and only != (hh, t)):
                    continue
                m_old = m_scr[hh, t]
                m_new = jnp.maximum(m_old, mx_scr[slot, hh, t])
                p = jnp.exp2(s_scr[slot, hh, t] - m_new).astype(BF16)
                alpha = jnp.exp2(m_old - m_new)
                acc_scr[hh, t] = alpha * acc_scr[hh, t] + jnp.dot(vt, p, preferred_element_type=F32)
                m_scr[hh, t] = m_new

    def finalize(r0):
        for t in range(n_sub):
            outs = []
            for hh in range(2):
                a = fin_scr[hh, t]
                outs.append((a[:V_HEAD, :] / a[V_HEAD:V_HEAD + 1, :]).T)
            o_ref[0, pl.ds(r0 + t * Q_BLOCK, Q_BLOCK), :] = jnp.concatenate(outs, axis=-1).astype(BF16)

    def group(gi, carry):
        r0 = pl.multiple_of(gi * SEQ_TILE, SEQ_TILE)
        qb0 = gi * (SEQ_TILE // KV_BLOCK)
        n_full = gi * n_diag
        chains = [(hh, t) for hh in range(2) for t in range(n_sub)]
        acc_scr[...] = jnp.zeros(acc_scr.shape, F32)
        m_scr[...] = jnp.full(m_scr.shape, NEG_BIG, F32)

        diag_order = list(range(n_diag - 1, -1, -1))
        scores(qb0, n_full + diag_order[0], 0, diag_order[0])
        finalize(pl.multiple_of(jnp.maximum(gi - 1, 0) * SEQ_TILE, SEQ_TILE))
        for pos in range(1, n_diag):
            scores(qb0, n_full + diag_order[pos], pos, diag_order[pos])
            update(n_full + diag_order[pos - 1], pos - 1, diag_order[pos - 1])

        def kv_trip(i):
            kb = i * n_diag
            for c in chains:
                scores(qb0, kb, 0, None, c)
                update(jnp.where(i == 0, n_full, kb - 1), n_diag - 1, None, c)
            for j in range(1, n_diag):
                for c in chains:
                    scores(qb0, kb + j, j, None, c)
                    update(kb + j - 1, j - 1, None, c)

        def kv_trips(ii, c):
            for u in range(ATTN_UNROLL):
                kv_trip(ii * ATTN_UNROLL + u)
            return c

        lax.fori_loop(0, gi // ATTN_UNROLL, kv_trips, 0)
        done = (gi // ATTN_UNROLL) * ATTN_UNROLL
        for left in range(1, ATTN_UNROLL):
            @pl.when(gi - done == left)
            def _():
                for u in range(left):
                    kv_trip(done + u)
        update(jnp.where(gi == 0, 0, n_full - 1), n_diag - 1, None)
        fin_scr[...] = acc_scr[...]
        return carry

    fin_scr[...] = jnp.ones(fin_scr.shape, F32)
    lax.fori_loop(0, n_groups, group, 0)
    finalize((n_groups - 1) * SEQ_TILE)


def _attn_call(qt, k, vt):
    b, nh, lp, _ = k.shape
    k_spec = pl.BlockSpec((1, 2, lp, LANES), lambda bi, hp: (bi, hp, 0, 0))
    blk_spec = lambda a: pl.BlockSpec((1, 2) + a.shape[2:], lambda bi, hp: (bi, hp, 0, 0, 0))
    n_sub = SEQ_TILE // Q_BLOCK
    return pl.pallas_call(
        functools.partial(_attn_body, n_groups=lp // SEQ_TILE),
        grid=(b, nh // 2),
        in_specs=[blk_spec(qt), k_spec, blk_spec(vt)],
        out_specs=pl.BlockSpec((1, lp, 2 * V_HEAD), lambda bi, hp: (bi, 0, hp)),
        out_shape=jax.ShapeDtypeStruct((b, lp, nh * V_HEAD), BF16),
        scratch_shapes=[pltpu.VMEM((2, n_sub, VT_ROWS, Q_BLOCK), F32),
                        pltpu.VMEM((2, n_sub, 1, Q_BLOCK), F32),
                        pltpu.VMEM((SEQ_TILE // K_TILE, 2, n_sub, K_TILE, Q_BLOCK), F32),
                        pltpu.VMEM((SEQ_TILE // K_TILE, 2, n_sub, 1, Q_BLOCK), F32),
                        pltpu.VMEM((2, n_sub, VT_ROWS, Q_BLOCK), F32)],
        compiler_params=_params(("parallel", "parallel")),
        name="attn",
    )(qt, k, vt)


def _split3_bf16(x):
    hi = x.astype(BF16)
    r1 = x - hi.astype(F32)
    mid = r1.astype(BF16)
    lo = (r1 - mid.astype(F32)).astype(BF16)
    return hi, mid, lo


def _hgrn_levels():
    w, out = HG_CHUNK // 2, []
    while w >= 1:
        out.append(w)
        w //= 2
    return out


def _hgrn_sum_matrix():
    c = HG_CHUNK
    t = np.arange(c)[:, None]
    s = np.arange(c)[None, :]
    blocks = [s <= t, s > t]
    for w in _hgrn_levels():
        r = (t // (2 * w)) * (2 * w) + w - 1
        is_q = (t // w) % 2 == 1
        blocks.append(np.where(is_q, (s > r) & (s <= t), (s > t) & (s <= r)))
    m = np.concatenate(blocks, axis=0).astype(np.float32)
    return jnp.asarray(np.concatenate([m, m, m], axis=1), dtype=BF16)


def _hgrn_body(hh_ref, lbraw_ref, nw_ref, p3_ref, o_ref, st_scr, *, layer):
    c = HG_CHUNK
    hw = HG_HEADS * HG_DIM
    n_chunks = hh_ref.shape[1] // c
    levels = _hgrn_levels()

    @pl.when(pl.program_id(1) == 0)
    def _():
        st_scr[...] = jnp.zeros(st_scr.shape, F32)

    raw = lbraw_ref[...]
    depth = raw.shape[0]
    rows = [raw[j:j + 1, :] for j in range(depth)]
    mx = functools.reduce(jnp.maximum, rows)
    ex = [jnp.exp(r - mx) for r in rows]
    inv_den = 1.0 / functools.reduce(jnp.add, ex)
    lb = jnp.zeros_like(mx)
    for j in range(1, layer + 1):
        lb = lb + ex[j] * inv_den
    one_m_lb = 1.0 - lb
    nw = nw_ref[...]

    ri = lax.broadcasted_iota(jnp.int32, (c, c), 0)
    ci = lax.broadcasted_iota(jnp.int32, (c, c), 1)
    rowc = lax.broadcasted_iota(jnp.int32, (c, 1), 0)
    diag = ri == ci
    is_q = [(rowc // w) % 2 == 1 for w in levels]
    same_parent = [(ri // (2 * w)) == (ci // (2 * w)) for w in levels]

    def prepare(r0):
        xq = hh_ref[0, pl.ds(r0, c), 0:hw]
        z = hh_ref[0, pl.ds(r0, c), hw:2 * hw]
        vin = hh_ref[0, pl.ds(r0, c), 2 * hw:3 * hw]

        e = jnp.exp(-jnp.abs(z))
        r = 1.0 / (1.0 + e)
        er = e * r
        pos = z >= 0
        f = lb + one_m_lb * jnp.where(pos, r, er)
        logf2 = jnp.log(jnp.maximum(f, F_MIN)) * LOG2E
        kk = one_m_lb * jnp.where(pos, er, r)
        qq = xq
        expo = jnp.dot(p3_ref[...], jnp.concatenate(_split3_bf16(logf2), axis=0),
                       preferred_element_type=F32)
        return qq, kk, vin.astype(BF16), jnp.exp2(expo)

    def products(qq, kk, vb, dec):
        heads = []
        for hd in range(HG_HEADS):
            sl = slice(hd * HG_DIM, (hd + 1) * HG_DIM)
            qh, kh, vh = qq[:, sl], kk[:, sl], vb[:, sl]
            blk = lambda i: dec[i * c:(i + 1) * c, sl]
            st = st_scr[hd]
            d_cum = blk(0)
            parts = [lax.dot_general(qh.astype(BF16), kh.astype(BF16), _NT, preferred_element_type=F32)]
            for i, w in enumerate(levels):
                d_w = blk(2 + i)
                qt = jnp.where(is_q[i], qh * d_w, 0.0).astype(BF16)
                kt = jnp.where(is_q[i], 0.0, kh * d_w).astype(BF16)
                parts.append(lax.dot_general(qt, kt, _NT, preferred_element_type=F32))
            qe = (qh * d_cum).astype(BF16)
            o_st = lax.dot_general(qe, st.astype(BF16), _NT, preferred_element_type=F32)
            kd = (kh * blk(1)).astype(BF16)
            st_scr[hd] = st * d_cum[c - 1:c, :] + lax.dot_general(vh, kd, _TN, preferred_element_type=F32)
            heads.append((sl, vh, parts, o_st))
        return heads

    def finish(r0, heads):
        for sl, vh, parts, o_st in heads:
            a = jnp.where(diag, parts[0], 0.0)
            for i, w in enumerate(levels):
                a = a + (parts[1 + i] if 2 * w == c else jnp.where(same_parent[i], parts[1 + i], 0.0))
            o_acc = o_st + jnp.dot(a.astype(BF16), vh, preferred_element_type=F32)
            gh = hh_ref[0, pl.ds(r0, c), 3 * hw + sl.start:3 * hw + sl.stop]
            y = _rms(o_acc, nw) * gh
            o_ref[0, pl.ds(r0, c), sl] = y.astype(BF16)

    def chunks(it, carry):
        starts = [pl.multiple_of((it * HG_UNROLL + j) * c, c) for j in range(HG_UNROLL)]
        prepared = [prepare(r0) for r0 in starts]
        produced = [products(*pr) for pr in prepared]
        for r0, heads in zip(starts, produced):
            finish(r0, heads)
        return carry

    lax.fori_loop(0, n_chunks // HG_UNROLL, chunks, 0)


def _hgrn_call(hh, lb_raw, nw, layer):
    b, lp, cols = hh.shape
    hw = HG_HEADS * HG_DIM
    t = SEQ_TILE
    p3 = _hgrn_sum_matrix()
    return pl.pallas_call(
        functools.partial(_hgrn_body, layer=layer),
        grid=(b, lp // t),
        in_specs=[pl.BlockSpec((1, t, cols), lambda bi, i: (bi, i, 0)),
                  _const_spec(lb_raw), _const_spec(nw), _const_spec(p3)],
        out_specs=pl.BlockSpec((1, t, hw), lambda bi, i: (bi, i, 0)),
        out_shape=jax.ShapeDtypeStruct((b, lp, hw), BF16),
        scratch_shapes=[pltpu.VMEM((HG_HEADS, HG_DIM, HG_DIM), F32)],
        compiler_params=_params(("parallel", "arbitrary")),
        name="hgrn2",
    )(hh, lb_raw, _operand(nw), p3)


def _rope_tables(lp):
    half = QK_ROPE // 2
    inv = ROPE_THETA ** (-jnp.arange(half, dtype=F32) / half)
    ang = jnp.arange(lp).astype(F32)[:, None] * inv[None, :]
    cos, sin = jnp.cos(ang), jnp.sin(ang)
    zpad = jnp.zeros((lp, LANES - QK_NOPE - QK_ROPE), F32)
    sin_s = jnp.concatenate([jnp.zeros((lp, QK_NOPE), F32), -sin, sin, zpad], axis=1)
    cos_k = jnp.concatenate([jnp.zeros((lp, QK_NOPE), F32), cos, cos, zpad], axis=1)
    return (cos * Q_SCALE).T, (sin * Q_SCALE).T, cos_k, sin_s


def _swap_halves(w):
    half = w.shape[-1] // 2
    return jnp.concatenate([w[..., half:], w[..., :half]], axis=-1)


def _prep_mixer_weights(w_in, w_uq, w_ukv, q_lora, kv_lora):
    depth, d, _ = w_in.shape
    hg = HG_HEADS * HG_DIM
    o_pe = q_lora + kv_lora
    o_hg = o_pe + QK_ROPE
    w_pe = w_in[:, :, o_pe:o_hg]
    zl = jnp.zeros((depth, d, QK_NOPE), F32)
    zr = jnp.zeros((depth, d, LANES - QK_NOPE - QK_ROPE), F32)
    wu = jnp.concatenate([w_in[:, :, :o_pe], zl, w_pe, zr, zl, _swap_halves(w_pe), zr], axis=2).astype(BF16)
    wb = w_in[:, :, o_hg:].astype(BF16)

    dqk = QK_NOPE + QK_ROPE
    uq = w_uq.reshape(depth, q_lora, MLA_HEADS, dqk)
    zq = jnp.zeros((depth, q_lora, MLA_HEADS, LANES - dqk), F32)
    wuqt = jnp.swapaxes(jnp.concatenate([uq, zq], axis=3).reshape(depth, q_lora, -1), 1, 2).astype(BF16)

    ukv = w_ukv.reshape(depth, kv_lora, MLA_HEADS, QK_NOPE + V_HEAD)
    zk = jnp.zeros((depth, kv_lora, MLA_HEADS, LANES - QK_NOPE), F32)
    zv = jnp.zeros((depth, kv_lora, MLA_HEADS, VT_ROWS - V_HEAD), F32)
    wk = jnp.concatenate([ukv[..., :QK_NOPE], zk], axis=3).reshape(depth, kv_lora, -1).astype(BF16)
    wv = jnp.concatenate([ukv[..., QK_NOPE:], zv], axis=3).reshape(depth, kv_lora, -1)
    wvt = jnp.swapaxes(wv, 1, 2).astype(BF16)
    return wu, wb, wuqt, wk, wvt, 4 * hg


def kernel(x, meta_tokens, ffn1_norm, ffn1_w_gu, ffn1_w_down, mix_norm, w_in, q_norm, kv_norm,
           w_uq, w_ukv, hg_lb_raw, hg_norm, w_proj_attn, w_proj_rec, w_out,
           ffn2_norm, ffn2_w_gu, ffn2_w_down, final_norm):
    b, seq, d = x.shape
    depth = w_in.shape[0]
    q_lora, kv_lora = q_norm.shape[1], kv_norm.shape[1]
    l = N_META + seq
    lp = -(-l // SEQ_TILE) * SEQ_TILE
    assert (b * lp) % ROW_TILE == 0 and lp % MIX_TILE == 0

    meta = jnp.broadcast_to(meta_tokens.astype(x.dtype)[None], (b, N_META, d))
    h = jnp.concatenate([meta, x, jnp.zeros((b, lp - l, d), x.dtype)], axis=1)

    wgu1, wgu2 = ffn1_w_gu.astype(BF16), ffn2_w_gu.astype(BF16)
    wd1, wd2 = (0.5 * ffn1_w_down).astype(BF16), (0.5 * ffn2_w_down).astype(BF16)
    wpa, wpr, wo = w_proj_attn.astype(BF16), w_proj_rec.astype(BF16), w_out.astype(BF16)
    wu, wb, wuqt, wk, wvt, hg_cols = _prep_mixer_weights(w_in, w_uq, w_ukv, q_lora, kv_lora)
    tabs = _rope_tables(lp)
    row2 = lambda a: a[:, None, :]
    n1, nm, n2, qn, kvn, hgn = map(row2, (ffn1_norm, mix_norm, ffn2_norm, q_norm, kv_norm, hg_norm))
    fnw = final_norm[None, :]
    lb_raw = hg_lb_raw.astype(F32)

    h2 = h.reshape(b * lp, d)
    for layer in range(depth):
        at = lambda a: _Layer(a, layer)
        h2 = _ffn_call(h2, at(n1), at(wgu1), at(wd1))
        qt, k, vt, hh, gates = _mix_in_call(h2.reshape(b, lp, d), at(nm), at(wu), at(wb), at(qn), at(kvn),
                                            at(wuqt), at(wk), at(wvt), tabs, hg_cols)
        o_attn = _attn_call(qt, k, vt)
        o_rec = _hgrn_call(hh, lb_raw, at(hgn), layer)
        tail = (at(wpa), at(wpr), at(wo), at(n2), at(wgu2), at(wd2), fnw)
        if layer == depth - 1 and seq % ROW_TILE == 0:
            return _merge_ffn_out_call(h2.reshape(b, lp, d), o_attn, o_rec, gates, *tail,
                                       first=N_META, count=seq)
        h2 = _merge_ffn_call(h2, o_attn.reshape(b * lp, -1), o_rec.reshape(b * lp, -1),
                             gates.reshape(b * lp, -1), *tail, final=(layer == depth - 1))
    return h2.reshape(b, lp, d)[:, N_META:l]
```

```python
import functools
import math
from typing import NamedTuple

import numpy as np
import jax
import jax.numpy as jnp
from jax import lax
from jax.experimental import pallas as pl
from jax.experimental.pallas import tpu as pltpu

F32 = jnp.float32
BF16 = jnp.bfloat16

N_META = 16
MLA_HEADS = 8
QK_NOPE = 64
QK_ROPE = 32
V_HEAD = 64
ROPE_THETA = 10000.0
HG_HEADS = 4
HG_DIM = 128
EPS = 1e-6
F_MIN = 1e-20
NEG_BIG = -1e30
LOG2E = math.log2(math.e)
Q_SCALE = LOG2E * (QK_NOPE + QK_ROPE) ** -0.5

LANES = 128
SUBLANES = 8
MXU_TILE = 256
VMEM_LIMIT_BYTES = 56 * 1024 * 1024

SEQ_TILE = 768
Q_BLOCK = 256
K_TILE = 256
KV_BLOCK = 128
ATTN_UNROLL = 4
VT_ROWS = 80
MIX_TILE = 384
ROW_TILE = 512
HG_CHUNK = 64
HG_UNROLL = 2
FFN_CHUNKS = 2
FFN_SPLIT = 2

_NT = (((1,), (1,)), ((), ()))
_TN = (((0,), (0,)), ((), ()))


def _sigmoid(x):
    return 1.0 / (1.0 + jnp.exp(-x))


def _rms(x, w):
    return x * lax.rsqrt(jnp.mean(x * x, axis=-1, keepdims=True) + EPS) * w


class _Layer(NamedTuple):
    stacked: jax.Array
    layer: int

    @property
    def shape(self):
        return self.stacked.shape[1:]


def _operand(p):
    return p.stacked if isinstance(p, _Layer) else p


def _const_spec(p):
    nd = len(p.shape)
    if isinstance(p, _Layer):
        return pl.BlockSpec((None,) + tuple(p.shape), lambda *_: (p.layer,) + (0,) * nd,
                            pipeline_mode=pl.Buffered(1))
    return pl.BlockSpec(tuple(p.shape), lambda *_: (0,) * nd, pipeline_mode=pl.Buffered(1))


def _params(sem):
    return pltpu.CompilerParams(dimension_semantics=sem, vmem_limit_bytes=VMEM_LIMIT_BYTES)


def _ffn_chunks(d_ff):
    n_tiles = d_ff // MXU_TILE
    bounds = [round(i * n_tiles / FFN_CHUNKS) * MXU_TILE for i in range(FFN_CHUNKS + 1)]
    return list(zip(bounds[:-1], bounds[1:]))


def _row_groups(n_rows):
    rows = n_rows // FFN_SPLIT
    return [slice(i * rows, (i + 1) * rows) for i in range(FFN_SPLIT)]


def _ffn_value(xs, nw, wgu_ref, wd_ref):
    d_ff = wd_ref.shape[0]
    accs = list(xs)
    xns = [_rms(a, nw).astype(BF16) for a in accs]
    for lo, hi in _ffn_chunks(d_ff):
        gs = [jnp.dot(xn, wgu_ref[:, lo:hi], preferred_element_type=F32) for xn in xns]
        us = [jnp.dot(xn, wgu_ref[:, d_ff + lo:d_ff + hi], preferred_element_type=F32) for xn in xns]
        acts = [(g * _sigmoid(g) * u).astype(BF16) for g, u in zip(gs, us)]
        accs = [acc + jnp.dot(a, wd_ref[lo:hi, :], preferred_element_type=F32) for acc, a in zip(accs, acts)]
    return jnp.concatenate(accs, axis=0)


def _ffn_body(x_ref, nw_ref, wgu_ref, wd_ref, o_ref):
    xs = [x_ref[g, :] for g in _row_groups(x_ref.shape[0])]
    o_ref[...] = _ffn_value(xs, nw_ref[...], wgu_ref, wd_ref)


def _ffn_call(h2d, nw, wgu, wd):
    rows, d = h2d.shape
    return pl.pallas_call(
        _ffn_body,
        grid=(rows // ROW_TILE,),
        in_specs=[
            pl.BlockSpec((ROW_TILE, d), lambda i: (i, 0)),
            _const_spec(nw),
            _const_spec(wgu),
            _const_spec(wd),
        ],
        out_specs=pl.BlockSpec((ROW_TILE, d), lambda i: (i, 0)),
        out_shape=jax.ShapeDtypeStruct((rows, d), F32),
        compiler_params=_params(("parallel",)),
        name="ffn",
    )(h2d, _operand(nw), _operand(wgu), _operand(wd))


def _merge_ffn_body(h_ref, oa_ref, ob_ref, g_ref, wpa_ref, wpr_ref, wo_ref,
                    nw_ref, wgu_ref, wd_ref, fnw_ref, o_ref, *, final):
    d = h_ref.shape[1]
    groups = _row_groups(h_ref.shape[0])
    yas = [jnp.dot(oa_ref[g, :], wpa_ref[...], preferred_element_type=F32) for g in groups]
    ybs = [jnp.dot(ob_ref[g, :], wpr_ref[...], preferred_element_type=F32) for g in groups]
    merged = [(g_ref[g, :d].astype(F32) * ya + g_ref[g, d:].astype(F32) * yb).astype(BF16)
              for g, ya, yb in zip(groups, yas, ybs)]
    hs = [h_ref[g, :] + jnp.dot(m, wo_ref[...], preferred_element_type=F32) for g, m in zip(groups, merged)]
    h = _ffn_value(hs, nw_ref[...], wgu_ref, wd_ref)
    if final:
        h = _rms(h, fnw_ref[...])
    o_ref[...] = h


def _merge_ffn_call(h2d, oa, ob, gates, wpa, wpr, wo, nw, wgu, wd, fnw, final):
    rows, d = h2d.shape
    row_spec = lambda a: pl.BlockSpec((ROW_TILE, a.shape[1]), lambda i: (i, 0))
    consts = (wpa, wpr, wo, nw, wgu, wd, fnw)
    return pl.pallas_call(
        functools.partial(_merge_ffn_body, final=final),
        grid=(rows // ROW_TILE,),
        in_specs=[row_spec(h2d), row_spec(oa), row_spec(ob), row_spec(gates)]
                 + [_const_spec(c) for c in consts],
        out_specs=pl.BlockSpec((ROW_TILE, d), lambda i: (i, 0)),
        out_shape=jax.ShapeDtypeStruct((rows, d), F32),
        compiler_params=_params(("parallel",)),
        name="merge_ffn",
    )(h2d, oa, ob, gates, *map(_operand, consts))


def _merge_ffn_out_call(h3d, oa, ob, gates, wpa, wpr, wo, nw, wgu, wd, fnw, first, count):
    b, _, d = h3d.shape
    row_spec = lambda a: pl.BlockSpec((pl.Element(1), pl.Element(ROW_TILE), pl.Element(a.shape[2])),
                                      lambda bi, i: (bi, pl.multiple_of(first + i * ROW_TILE, math.gcd(first, ROW_TILE)), 0))
    consts = (wpa, wpr, wo, nw, wgu, wd, fnw)

    def body(h_ref, oa_ref, ob_ref, g_ref, *rest):
        _merge_ffn_body(h_ref.at[0], oa_ref.at[0], ob_ref.at[0], g_ref.at[0], *rest, final=True)

    return pl.pallas_call(
        body,
        grid=(b, count // ROW_TILE),
        in_specs=[row_spec(h3d), row_spec(oa), row_spec(ob), row_spec(gates)]
                 + [_const_spec(c) for c in consts],
        out_specs=pl.BlockSpec((None, ROW_TILE, d), lambda bi, i: (bi, i, 0)),
        out_shape=jax.ShapeDtypeStruct((b, count, d), F32),
        compiler_params=_params(("parallel", "parallel")),
        name="merge_ffn_out",
    )(h3d, oa, ob, gates, *map(_operand, consts))


def _mix_in_body(h_ref, nw_ref, wu_ref, wb_ref, qn_ref, kvn_ref, wuqt_ref, wk_ref, wvt_ref,
                 cqt_ref, sqt_ref, ck_ref, sk_ref,
                 qt_ref, k_ref, vt_ref, hh_ref, g_ref, *, q_lora, kv_lora):
    o_kv = q_lora
    o_pe = o_kv + kv_lora
    o_gt = hh_ref.shape[2]
    u = _rms(h_ref[0], nw_ref[...]).astype(BF16)

    cq = jnp.dot(u, wu_ref[:, :o_kv], preferred_element_type=F32)
    ckv = jnp.dot(u, wu_ref[:, o_kv:o_pe], preferred_element_type=F32)
    kpe = jnp.dot(u, wu_ref[:, o_pe:], preferred_element_type=F32)
    cqn = _rms(cq, qn_ref[...]).astype(BF16)
    ckvn = _rms(ckv, kvn_ref[...]).astype(BF16)

    hg = jnp.dot(u, wb_ref[:, :o_gt], preferred_element_type=F32)
    hw = o_gt // 4
    xq, xg = hg[:, :hw], hg[:, 3 * hw:]
    hh_ref[0, :, :hw] = xq * _sigmoid(xq) * (HG_DIM ** -0.5)
    hh_ref[0, :, hw:3 * hw] = hg[:, hw:3 * hw]
    hh_ref[0, :, 3 * hw:] = xg * _sigmoid(xg)
    g_ref[0] = _sigmoid(jnp.dot(u, wb_ref[:, o_gt:], preferred_element_type=F32)).astype(BF16)

    qft = lax.dot_general(wuqt_ref[...], cqn, _NT, preferred_element_type=F32)
    cq_t, sq_t = cqt_ref[...], sqt_ref[...]
    half = QK_ROPE // 2
    for hd in range(MLA_HEADS):
        blk = qft[hd * LANES:(hd + 1) * LANES]
        x1, x2 = blk[QK_NOPE:QK_NOPE + half], blk[QK_NOPE + half:QK_NOPE + QK_ROPE]
        qh = jnp.concatenate([blk[:QK_NOPE] * Q_SCALE, x1 * cq_t - x2 * sq_t, x2 * cq_t + x1 * sq_t,
                              blk[QK_NOPE + QK_ROPE:]], axis=0).astype(BF16)
        for jj in range(qh.shape[1] // KV_BLOCK):
            qt_ref[0, hd, jj] = qh[:, jj * KV_BLOCK:(jj + 1) * KV_BLOCK]

    krot = kpe[:, :LANES] * ck_ref[...] + kpe[:, LANES:] * sk_ref[...]
    kf = jnp.dot(ckvn, wk_ref[...], preferred_element_type=F32)
    for hd in range(MLA_HEADS):
        sl = slice(hd * LANES, (hd + 1) * LANES)
        k_ref[0, hd] = (kf[:, sl] + krot).astype(BF16)
    vt = lax.dot_general(wvt_ref[...], ckvn, _NT, preferred_element_type=F32)
    ones_row = (lax.broadcasted_iota(jnp.int32, (vt.shape[0], 1), 0) % VT_ROWS == V_HEAD).astype(F32)
    vt = (vt + ones_row).astype(BF16)
    for hd in range(MLA_HEADS):
        for jj in range(vt.shape[1] // KV_BLOCK):
            vt_ref[0, hd, jj] = vt[hd * VT_ROWS:(hd + 1) * VT_ROWS, jj * KV_BLOCK:(jj + 1) * KV_BLOCK]


def _mix_in_call(h3d, nw, wu, wb, qn, kvn, wuqt, wk, wvt, tabs, hg_cols):
    b, lp, d = h3d.shape
    tm = MIX_TILE
    head_shape = jax.ShapeDtypeStruct((b, MLA_HEADS, lp, LANES), BF16)
    head_spec = pl.BlockSpec((1, MLA_HEADS, tm, LANES), lambda bi, i: (bi, 0, i, 0))
    tab_spec = pl.BlockSpec((tm, LANES), lambda bi, i: (i, 0))
    tabt_spec = pl.BlockSpec((QK_ROPE // 2, tm), lambda bi, i: (0, i))
    qt_spec = pl.BlockSpec((1, MLA_HEADS, tm // KV_BLOCK, LANES, KV_BLOCK), lambda bi, i: (bi, 0, i, 0, 0))
    gate_cols = wb.shape[1] - hg_cols
    return pl.pallas_call(
        functools.partial(_mix_in_body, q_lora=qn.shape[1], kv_lora=kvn.shape[1]),
        grid=(b, lp // tm),
        in_specs=[pl.BlockSpec((1, tm, d), lambda bi, i: (bi, i, 0))]
                 + [_const_spec(c) for c in (nw, wu, wb, qn, kvn, wuqt, wk, wvt)]
                 + [tabt_spec] * 2 + [tab_spec] * 2,
        out_specs=[qt_spec, head_spec,
                   pl.BlockSpec((1, MLA_HEADS, tm // KV_BLOCK, VT_ROWS, KV_BLOCK),
                                lambda bi, i: (bi, 0, i, 0, 0)),
                   pl.BlockSpec((1, tm, hg_cols), lambda bi, i: (bi, i, 0)),
                   pl.BlockSpec((1, tm, gate_cols), lambda bi, i: (bi, i, 0))],
        out_shape=[jax.ShapeDtypeStruct((b, MLA_HEADS, lp // KV_BLOCK, LANES, KV_BLOCK), BF16), head_shape,
                   jax.ShapeDtypeStruct((b, MLA_HEADS, lp // KV_BLOCK, VT_ROWS, KV_BLOCK), BF16),
                   jax.ShapeDtypeStruct((b, lp, hg_cols), F32),
                   jax.ShapeDtypeStruct((b, lp, gate_cols), BF16)],
        compiler_params=_params(("parallel", "parallel")),
        name="mix_in",
    )(h3d, *map(_operand, (nw, wu, wb, qn, kvn, wuqt, wk, wvt)), *tabs)


def _attn_body(qt_ref, k_ref, vt_ref, o_ref, acc_scr, m_scr, s_scr, mx_scr, fin_scr, *, n_groups):
    n_sub = SEQ_TILE // Q_BLOCK
    n_diag = SEQ_TILE // K_TILE
    vt_per_tile = K_TILE // KV_BLOCK
    q_per_blk = Q_BLOCK // KV_BLOCK
    key_i = lax.broadcasted_iota(jnp.int32, (K_TILE, Q_BLOCK), 0)
    qry_i = lax.broadcasted_iota(jnp.int32, (K_TILE, Q_BLOCK), 1)

    def needed(j, t):
        return j * K_TILE < (t + 1) * Q_BLOCK

    def scores(qb0, kb, slot, diag_j, only=None):
        for hh in range(2):
            k = k_ref[0, hh, pl.ds(pl.multiple_of(kb * K_TILE, K_TILE), K_TILE), :]
            for t in range(n_sub):
                if (diag_j is not None and not needed(diag_j, t)) or (only is not None and only != (hh, t)):
                    continue
                qt = jnp.concatenate([qt_ref[0, hh, qb0 + t * q_per_blk + i] for i in range(q_per_blk)], axis=1)
                st = jnp.dot(k, qt, preferred_element_type=F32)
                if diag_j is not None and (diag_j + 1) * K_TILE > t * Q_BLOCK + 1:
                    st = jnp.where(key_i + diag_j * K_TILE <= qry_i + t * Q_BLOCK, st, NEG_BIG)
                s_scr[slot, hh, t] = st
                mx_scr[slot, hh, t] = jnp.max(st, axis=0, keepdims=True)

    def update(kb, slot, diag_j, only=None):
        for hh in range(2):
            vt = jnp.concatenate([vt_ref[0, hh, kb * vt_per_tile + i] for i in range(vt_per_tile)], axis=1)
            for t in range(n_sub):
                if (diag_j is not None and not needed(diag_j, t)) or (only is not None and only != (hh, t)):
                    continue
                m_old = m_scr[hh, t]
                m_new = jnp.maximum(m_old, mx_scr[slot, hh, t])
                p = jnp.exp2(s_scr[slot, hh, t] - m_new).astype(BF16)
                alpha = jnp.exp2(m_old - m_new)
                acc_scr[hh, t] = alpha * acc_scr[hh, t] + jnp.dot(vt, p, preferred_element_type=F32)
                m_scr[hh, t] = m_new

    def finalize(r0):
        for t in range(n_sub):
            outs = []
            for hh in range(2):
                a = fin_scr[hh, t]
                outs.append((a[:V_HEAD, :] / a[V_HEAD:V_HEAD + 1, :]).T)
            o_ref[0, pl.ds(r0 + t * Q_BLOCK, Q_BLOCK), :] = jnp.concatenate(outs, axis=-1).astype(BF16)

    def group(gi, carry):
        r0 = pl.multiple_of(gi * SEQ_TILE, SEQ_TILE)
        qb0 = gi * (SEQ_TILE // KV_BLOCK)
        n_full = gi * n_diag
        chains = [(hh, t) for hh in range(2) for t in range(n_sub)]
        acc_scr[...] = jnp.zeros(acc_scr.shape, F32)
        m_scr[...] = jnp.full(m_scr.shape, NEG_BIG, F32)

        diag_order = list(range(n_diag - 1, -1, -1))
        scores(qb0, n_full + diag_order[0], 0, diag_order[0])
        finalize(pl.multiple_of(jnp.maximum(gi - 1, 0) * SEQ_TILE, SEQ_TILE))
        for pos in range(1, n_diag):
            scores(qb0, n_full + diag_order[pos], pos, diag_order[pos])
            update(n_full + diag_order[pos - 1], pos - 1, diag_order[pos - 1])

        def kv_trip(i):
            kb = i * n_diag
            for c in chains:
                scores(qb0, kb, 0, None, c)
                update(jnp.where(i == 0, n_full, kb - 1), n_diag - 1, None, c)
            for j in range(1, n_diag):
                for c in chains:
                    scores(qb0, kb + j, j, None, c)
                    update(kb + j - 1, j - 1, None, c)

        def kv_trips(ii, c):
            for u in range(ATTN_UNROLL):
                kv_trip(ii * ATTN_UNROLL + u)
            return c

        lax.fori_loop(0, gi // ATTN_UNROLL, kv_trips, 0)
        done = (gi // ATTN_UNROLL) * ATTN_UNROLL
        for left in range(1, ATTN_UNROLL):
            @pl.when(gi - done == left)
            def _():
                for u in range(left):
                    kv_trip(done + u)
        update(jnp.where(gi == 0, 0, n_full - 1), n_diag - 1, None)
        fin_scr[...] = acc_scr[...]
        return carry

    fin_scr[...] = jnp.ones(fin_scr.shape, F32)
    lax.fori_loop(0, n_groups, group, 0)
    finalize((n_groups - 1) * SEQ_TILE)


def _attn_call(qt, k, vt):
    b, nh, lp, _ = k.shape
    k_spec = pl.BlockSpec((1, 2, lp, LANES), lambda bi, hp: (bi, hp, 0, 0))
    blk_spec = lambda a: pl.BlockSpec((1, 2) + a.shape[2:], lambda bi, hp: (bi, hp, 0, 0, 0))
    n_sub = SEQ_TILE // Q_BLOCK
    return pl.pallas_call(
        functools.partial(_attn_body, n_groups=lp // SEQ_TILE),
        grid=(b, nh // 2),
        in_specs=[blk_spec(qt), k_spec, blk_spec(vt)],
        out_specs=pl.BlockSpec((1, lp, 2 * V_HEAD), lambda bi, hp: (bi, 0, hp)),
        out_shape=jax.ShapeDtypeStruct((b, lp, nh * V_HEAD), BF16),
        scratch_shapes=[pltpu.VMEM((2, n_sub, VT_ROWS, Q_BLOCK), F32),
                        pltpu.VMEM((2, n_sub, 1, Q_BLOCK), F32),
                        pltpu.VMEM((SEQ_TILE // K_TILE, 2, n_sub, K_TILE, Q_BLOCK), F32),
                        pltpu.VMEM((SEQ_TILE // K_TILE, 2, n_sub, 1, Q_BLOCK), F32),
                        pltpu.VMEM((2, n_sub, VT_ROWS, Q_BLOCK), F32)],
        compiler_params=_params(("parallel", "parallel")),
        name="attn",
    )(qt, k, vt)


def _split3_bf16(x):
    hi = x.astype(BF16)
    r1 = x - hi.astype(F32)
    mid = r1.astype(BF16)
    lo = (r1 - mid.astype(F32)).astype(BF16)
    return hi, mid, lo


def _hgrn_levels():
    w, out = HG_CHUNK // 2, []
    while w >= 1:
        out.append(w)
        w //= 2
    return out


def _hgrn_sum_matrix():
    c = HG_CHUNK
    t = np.arange(c)[:, None]
    s = np.arange(c)[None, :]
    blocks = [s <= t, s > t]
    for w in _hgrn_levels():
        r = (t // (2 * w)) * (2 * w) + w - 1
        is_q = (t // w) % 2 == 1
        blocks.append(np.where(is_q, (s > r) & (s <= t), (s > t) & (s <= r)))
    m = np.concatenate(blocks, axis=0).astype(np.float32)
    return jnp.asarray(np.concatenate([m, m, m], axis=1), dtype=BF16)


def _hgrn_body(hh_ref, lbraw_ref, nw_ref, p3_ref, o_ref, st_scr, *, layer):
    c = HG_CHUNK
    hw = HG_HEADS * HG_DIM
    n_chunks = hh_ref.shape[1] // c
    levels = _hgrn_levels()

    @pl.when(pl.program_id(1) == 0)
    def _():
        st_scr[...] = jnp.zeros(st_scr.shape, F32)

    raw = lbraw_ref[...]
    depth = raw.shape[0]
    rows = [raw[j:j + 1, :] for j in range(depth)]
    mx = functools.reduce(jnp.maximum, rows)
    ex = [jnp.exp(r - mx) for r in rows]
    inv_den = 1.0 / functools.reduce(jnp.add, ex)
    lb = jnp.zeros_like(mx)
    for j in range(1, layer + 1):
        lb = lb + ex[j] * inv_den
    one_m_lb = 1.0 - lb
    nw = nw_ref[...]

    ri = lax.broadcasted_iota(jnp.int32, (c, c), 0)
    ci = lax.broadcasted_iota(jnp.int32, (c, c), 1)
    rowc = lax.broadcasted_iota(jnp.int32, (c, 1), 0)
    diag = ri == ci
    is_q = [(rowc // w) % 2 == 1 for w in levels]
    same_parent = [(ri // (2 * w)) == (ci // (2 * w)) for w in levels]

    def prepare(r0):
        xq = hh_ref[0, pl.ds(r0, c), 0:hw]
        z = hh_ref[0, pl.ds(r0, c), hw:2 * hw]
        vin = hh_ref[0, pl.ds(r0, c), 2 * hw:3 * hw]

        e = jnp.exp(-jnp.abs(z))
        r = 1.0 / (1.0 + e)
        er = e * r
        pos = z >= 0
        f = lb + one_m_lb * jnp.where(pos, r, er)
        logf2 = jnp.log(jnp.maximum(f, F_MIN)) * LOG2E
        kk = one_m_lb * jnp.where(pos, er, r)
        qq = xq
        expo = jnp.dot(p3_ref[...], jnp.concatenate(_split3_bf16(logf2), axis=0),
                       preferred_element_type=F32)
        return qq, kk, vin.astype(BF16), jnp.exp2(expo)

    def products(qq, kk, vb, dec):
        heads = []
        for hd in range(HG_HEADS):
            sl = slice(hd * HG_DIM, (hd + 1) * HG_DIM)
            qh, kh, vh = qq[:, sl], kk[:, sl], vb[:, sl]
            blk = lambda i: dec[i * c:(i + 1) * c, sl]
            st = st_scr[hd]
            d_cum = blk(0)
            parts = [lax.dot_general(qh.astype(BF16), kh.astype(BF16), _NT, preferred_element_type=F32)]
            for i, w in enumerate(levels):
                d_w = blk(2 + i)
                qt = jnp.where(is_q[i], qh * d_w, 0.0).astype(BF16)
                kt = jnp.where(is_q[i], 0.0, kh * d_w).astype(BF16)
                parts.append(lax.dot_general(qt, kt, _NT, preferred_element_type=F32))
            qe = (qh * d_cum).astype(BF16)
            o_st = lax.dot_general(qe, st.astype(BF16), _NT, preferred_element_type=F32)
            kd = (kh * blk(1)).astype(BF16)
            st_scr[hd] = st * d_cum[c - 1:c, :] + lax.dot_general(vh, kd, _TN, preferred_element_type=F32)
            heads.append((sl, vh, parts, o_st))
        return heads

    def finish(r0, heads):
        for sl, vh, parts, o_st in heads:
            a = jnp.where(diag, parts[0], 0.0)
            for i, w in enumerate(levels):
                a = a + (parts[1 + i] if 2 * w == c else jnp.where(same_parent[i], parts[1 + i], 0.0))
            o_acc = o_st + jnp.dot(a.astype(BF16), vh, preferred_element_type=F32)
            gh = hh_ref[0, pl.ds(r0, c), 3 * hw + sl.start:3 * hw + sl.stop]
            y = _rms(o_acc, nw) * gh
            o_ref[0, pl.ds(r0, c), sl] = y.astype(BF16)

    def chunks(it, carry):
        starts = [pl.multiple_of((it * HG_UNROLL + j) * c, c) for j in range(HG_UNROLL)]
        prepared = [prepare(r0) for r0 in starts]
        produced = [products(*pr) for pr in prepared]
        for r0, heads in zip(starts, produced):
            finish(r0, heads)
        return carry

    lax.fori_loop(0, n_chunks // HG_UNROLL, chunks, 0)


def _hgrn_call(hh, lb_raw, nw, layer):
    b, lp, cols = hh.shape
    hw = HG_HEADS * HG_DIM
    t = SEQ_TILE
    p3 = _hgrn_sum_matrix()
    return pl.pallas_call(
        functools.partial(_hgrn_body, layer=layer),
        grid=(b, lp // t),
        in_specs=[pl.BlockSpec((1, t, cols), lambda bi, i: (bi, i, 0)),
                  _const_spec(lb_raw), _const_spec(nw), _const_spec(p3)],
        out_specs=pl.BlockSpec((1, t, hw), lambda bi, i: (bi, i, 0)),
        out_shape=jax.ShapeDtypeStruct((b, lp, hw), BF16),
        scratch_shapes=[pltpu.VMEM((HG_HEADS, HG_DIM, HG_DIM), F32)],
        compiler_params=_params(("parallel", "arbitrary")),
        name="hgrn2",
    )(hh, lb_raw, _operand(nw), p3)


def _rope_tables(lp):
    half = QK_ROPE // 2
    inv = ROPE_THETA ** (-jnp.arange(half, dtype=F32) / half)
    ang = jnp.arange(lp).astype(F32)[:, None] * inv[None, :]
    cos, sin = jnp.cos(ang), jnp.sin(ang)
    zpad = jnp.zeros((lp, LANES - QK_NOPE - QK_ROPE), F32)
    sin_s = jnp.concatenate([jnp.zeros((lp, QK_NOPE), F32), -sin, sin, zpad], axis=1)
    cos_k = jnp.concatenate([jnp.zeros((lp, QK_NOPE), F32), cos, cos, zpad], axis=1)
    return (cos * Q_SCALE).T, (sin * Q_SCALE).T, cos_k, sin_s


def _swap_halves(w):
    half = w.shape[-1] // 2
    return jnp.concatenate([w[..., half:], w[..., :half]], axis=-1)


def _prep_mixer_weights(w_in, w_uq, w_ukv, q_lora, kv_lora):
    depth, d, _ = w_in.shape
    hg = HG_HEADS * HG_DIM
    o_pe = q_lora + kv_lora
    o_hg = o_pe + QK_ROPE
    w_pe = w_in[:, :, o_pe:o_hg]
    zl = jnp.zeros((depth, d, QK_NOPE), F32)
    zr = jnp.zeros((depth, d, LANES - QK_NOPE - QK_ROPE), F32)
    wu = jnp.concatenate([w_in[:, :, :o_pe], zl, w_pe, zr, zl, _swap_halves(w_pe), zr], axis=2).astype(BF16)
    wb = w_in[:, :, o_hg:].astype(BF16)

    dqk = QK_NOPE + QK_ROPE
    uq = w_uq.reshape(depth, q_lora, MLA_HEADS, dqk)
    zq = jnp.zeros((depth, q_lora, MLA_HEADS, LANES - dqk), F32)
    wuqt = jnp.swapaxes(jnp.concatenate([uq, zq], axis=3).reshape(depth, q_lora, -1), 1, 2).astype(BF16)

    ukv = w_ukv.reshape(depth, kv_lora, MLA_HEADS, QK_NOPE + V_HEAD)
    zk = jnp.zeros((depth, kv_lora, MLA_HEADS, LANES - QK_NOPE), F32)
    zv = jnp.zeros((depth, kv_lora, MLA_HEADS, VT_ROWS - V_HEAD), F32)
    wk = jnp.concatenate([ukv[..., :QK_NOPE], zk], axis=3).reshape(depth, kv_lora, -1).astype(BF16)
    wv = jnp.concatenate([ukv[..., QK_NOPE:], zv], axis=3).reshape(depth, kv_lora, -1)
    wvt = jnp.swapaxes(wv, 1, 2).astype(BF16)
    return wu, wb, wuqt, wk, wvt, 4 * hg


def kernel(x, meta_tokens, ffn1_norm, ffn1_w_gu, ffn1_w_down, mix_norm, w_in, q_norm, kv_norm,
           w_uq, w_ukv, hg_lb_raw, hg_norm, w_proj_attn, w_proj_rec, w_out,
           ffn2_norm, ffn2_w_gu, ffn2_w_down, final_norm):
    b, seq, d = x.shape
    depth = w_in.shape[0]
    q_lora, kv_lora = q_norm.shape[1], kv_norm.shape[1]
    l = N_META + seq
    lp = -(-l // SEQ_TILE) * SEQ_TILE
    assert (b * lp) % ROW_TILE == 0 and lp % MIX_TILE == 0

    meta = jnp.broadcast_to(meta_tokens.astype(x.dtype)[None], (b, N_META, d))
    h = jnp.concatenate([meta, x, jnp.zeros((b, lp - l, d), x.dtype)], axis=1)

    wgu1, wgu2 = ffn1_w_gu.astype(BF16), ffn2_w_gu.astype(BF16)
    wd1, wd2 = (0.5 * ffn1_w_down).astype(BF16), (0.5 * ffn2_w_down).astype(BF16)
    wpa, wpr, wo = w_proj_attn.astype(BF16), w_proj_rec.astype(BF16), w_out.astype(BF16)
    wu, wb, wuqt, wk, wvt, hg_cols = _prep_mixer_weights(w_in, w_uq, w_ukv, q_lora, kv_lora)
    tabs = _rope_tables(lp)
    row2 = lambda a: a[:, None, :]
    n1, nm, n2, qn, kvn, hgn = map(row2, (ffn1_norm, mix_norm, ffn2_norm, q_norm, kv_norm, hg_norm))
    fnw = final_norm[None, :]
    lb_raw = hg_lb_raw.astype(F32)

    h2 = h.reshape(b * lp, d)
    for layer in range(depth):
        at = lambda a: _Layer(a, layer)
        h2 = _ffn_call(h2, at(n1), at(wgu1), at(wd1))
        qt, k, vt, hh, gates = _mix_in_call(h2.reshape(b, lp, d), at(nm), at(wu), at(wb), at(qn), at(kvn),
                                            at(wuqt), at(wk), at(wvt), tabs, hg_cols)
        o_attn = _attn_call(qt, k, vt)
        o_rec = _hgrn_call(hh, lb_raw, at(hgn), layer)
        tail = (at(wpa), at(wpr), at(wo), at(n2), at(wgu2), at(wd2), fnw)
        if layer == depth - 1 and seq % ROW_TILE == 0:
            return _merge_ffn_out_call(h2.reshape(b, lp, d), o_attn, o_rec, gates, *tail,
                                       first=N_META, count=seq)
        h2 = _merge_ffn_call(h2, o_attn.reshape(b * lp, -1), o_rec.reshape(b * lp, -1),
                             gates.reshape(b * lp, -1), *tail, final=(layer == depth - 1))
    return h2.reshape(b, lp, d)[:, N_META:l]
```
